```python
import math
import jax, jax.numpy as jnp
from jax import lax
import numpy as np

D_MODEL = 1024
BATCH = 2
SEQ = 8192
DEPTH = 2

HEAD_DIM = 64
GRID_W = 64
NQ_A = 4
NKV_A = 2
NQ_B = 4
NKV_B = 2
NH_C = 4
NQ_D = 4
NKV_D = 2
DIFF_DIM = HEAD_DIM // 2
BRANCH_W = 4 * HEAD_DIM
D_MIX = 4 * BRANCH_W
D_IN = (2 * NQ_A + 2 * NKV_A) * HEAD_DIM + (2 * NQ_B + 2 * NKV_B) * HEAD_DIM + 4 * NH_C * HEAD_DIM + (2 * NQ_D + 2 * NKV_D) * 2 * DIFF_DIM
Q_BLOCK = 128
WINDOW = 128
NA_KH = 8
NA_KW = 16
ROPE_THETA = 10000.0
EPS = 1e-6
NEG_INF = -1e30

kernel_name = "hybrid_parallel_heads_encoder"


def _in_sizes():
    hd, dd2 = HEAD_DIM, 2 * DIFF_DIM
    return [NQ_A * hd, NKV_A * hd, NKV_A * hd, NQ_A * hd,
            NQ_B * hd, NKV_B * hd, NKV_B * hd, NQ_B * hd,
            NH_C * hd, NH_C * hd, NH_C * hd, NH_C * hd,
            NQ_D * dd2, NKV_D * dd2, NKV_D * dd2, NQ_D * dd2]


def _rmsnorm(x, g):
    x32 = x.astype(jnp.float32)
    y = x32 * lax.rsqrt(jnp.mean(x32 * x32, axis=-1, keepdims=True) + EPS)
    return (y * g.astype(jnp.float32)).astype(x.dtype)


def _rope(x, pos):
    d = x.shape[-1]
    half = d // 2
    inv = jnp.power(ROPE_THETA, -jnp.arange(0, d, 2, dtype=jnp.float32) / d)
    ang = pos.astype(jnp.float32)[:, None] * inv[None, :]
    shp = (1, x.shape[1]) + (1,) * (x.ndim - 3) + (half,)
    cos = jnp.cos(ang).reshape(shp)
    sin = jnp.sin(ang).reshape(shp)
    x32 = x.astype(jnp.float32)
    x1, x2 = x32[..., :half], x32[..., half:]
    return jnp.concatenate([x1 * cos - x2 * sin, x2 * cos + x1 * sin], axis=-1).astype(x.dtype)


def _axial_rope(x, row, col):
    half = x.shape[-1] // 2
    return jnp.concatenate([_rope(x[..., :half], row), _rope(x[..., half:], col)], axis=-1)


def _dense_blocked(q, k, v, scale):
    b, s, hk, g, d = q.shape
    nb = s // Q_BLOCK
    qb = q.reshape(b, nb, Q_BLOCK, hk, g, d).transpose(1, 0, 2, 3, 4, 5)

    def body(qi):
        sc = jnp.einsum('bqkgd,bskd->bkgqs', qi, k).astype(jnp.float32) * scale
        p = jax.nn.softmax(sc, axis=-1).astype(v.dtype)
        return jnp.einsum('bkgqs,bskd->bqkgd', p, v)

    o = lax.map(body, qb)
    return o.transpose(1, 0, 2, 3, 4, 5).reshape(b, s, hk * g * v.shape[-1])


def _window_sink(q, k, v, sink, scale):
    b, s, hk, g, d = q.shape
    nb = s // Q_BLOCK
    pad = ((0, 0), (Q_BLOCK, Q_BLOCK), (0, 0), (0, 0))
    kp = jnp.pad(k, pad).reshape(b, nb + 2, Q_BLOCK, hk, d)
    vp = jnp.pad(v, pad).reshape(b, nb + 2, Q_BLOCK, hk, v.shape[-1])
    kw = jnp.concatenate([kp[:, :-2], kp[:, 1:-1], kp[:, 2:]], axis=2)
    vw = jnp.concatenate([vp[:, :-2], vp[:, 1:-1], vp[:, 2:]], axis=2)
    qb = q.reshape(b, nb, Q_BLOCK, hk, g, d)
    sc = jnp.einsum('bnqkgd,bnskd->bnkgqs', qb, kw).astype(jnp.float32) * scale
    blk = jnp.arange(nb)[:, None] * Q_BLOCK
    qpos = blk + jnp.arange(Q_BLOCK)[None, :]
    kpos = blk - Q_BLOCK + jnp.arange(3 * Q_BLOCK)[None, :]
    valid = ((jnp.abs(qpos[:, :, None] - kpos[:, None, :]) <= WINDOW)
             & (kpos[:, None, :] >= 0) & (kpos[:, None, :] < s))
    sc = jnp.where(valid[None, :, None, None], sc, NEG_INF)
    sk = jnp.broadcast_to(sink.astype(jnp.float32).reshape(1, 1, hk, g, 1, 1), sc.shape[:-1] + (1,))
    p = jax.nn.softmax(jnp.concatenate([sc, sk], axis=-1), axis=-1)[..., :-1].astype(v.dtype)
    o = jnp.einsum('bnkgqs,bnskd->bnqkgd', p, vw)
    return o.reshape(b, s, hk * g * v.shape[-1])


def _neighbourhood(q, k, v, rpb, scale):
    b, s, h, d = q.shape
    rows = s // GRID_W
    kh = min(NA_KH, rows)
    kw = NA_KW
    ncb = GRID_W // kw
    kc = 2 * kw
    nl = kh * kc
    r = jnp.arange(rows)
    rs = jnp.clip(r - kh // 2, 0, rows - kh)
    key_rows = rs[:, None] + jnp.arange(kh)[None, :]
    cb = jnp.arange(ncb)
    cbase = jnp.clip(cb * kw - kw // 2, 0, GRID_W - kc)
    key_cols = cbase[:, None] + jnp.arange(kc)[None, :]
    idx = (key_rows[:, None, :, None] * GRID_W + key_cols[None, :, None, :]).reshape(-1)
    kg = jnp.take(k, idx, axis=1).reshape(b, rows, ncb, nl, h, d)
    vg = jnp.take(v, idx, axis=1).reshape(b, rows, ncb, nl, h, v.shape[-1])
    qg = q.reshape(b, rows, ncb, kw, h, d)
    sc = jnp.einsum('brcqhd,brclhd->brchql', qg, kg).astype(jnp.float32) * scale
    qcol = cb[:, None] * kw + jnp.arange(kw)[None, :]
    cs = jnp.clip(qcol - kw // 2, 0, GRID_W - kw)
    kcol = key_cols[:, None, :]
    col_ok = (kcol >= cs[:, :, None]) & (kcol < cs[:, :, None] + kw)
    mask = jnp.broadcast_to(col_ok[:, :, None, :], (ncb, kw, kh, kc)).reshape(ncb, kw, nl)
    dr = key_rows - r[:, None] + NA_KH - 1
    dc = jnp.clip(kcol - qcol[:, :, None], -(kw - 1), kw - 1) + kw - 1
    bias = rpb.astype(jnp.float32)[:, dr[:, None, None, :, None], dc[None, :, :, None, :]]
    bias = bias.reshape(h, rows, ncb, kw, nl).transpose(1, 2, 0, 3, 4)
    sc = jnp.where(mask[None, None, :, None], sc + bias[None], NEG_INF)
    p = jax.nn.softmax(sc, axis=-1).astype(v.dtype)
    o = jnp.einsum('brchql,brclhd->brcqhd', p, vg)
    return o.reshape(b, s, h * v.shape[-1])


def _diff_blocked(q, k, v, lam, scale):
    b, s, hk, g, _, dd = q.shape
    nb = s // Q_BLOCK
    qb = q.reshape(b, nb, Q_BLOCK, hk, g, 2, dd).transpose(1, 0, 2, 3, 4, 5, 6)

    def body(qi):
        sc = jnp.einsum('bqkgcd,bskcd->cbkgqs', qi, k).astype(jnp.float32) * scale
        p = jax.nn.softmax(sc, axis=-1)
        w = (p[0] - lam * p[1]).astype(v.dtype)
        return jnp.einsum('bkgqs,bskd->bqkgd', w, v)

    o = lax.map(body, qb)
    return o.transpose(1, 0, 2, 3, 4, 5).reshape(b, s, hk, g, v.shape[-1])


def setup_inputs(seed: int = 0) -> dict:
    key = jax.random.key(seed)
    ks = jax.random.split(key, 14)
    f32 = jnp.float32
    nrm = lambda k, shp: jax.random.normal(k, shp, dtype=f32)
    return {
        "x": nrm(ks[0], (BATCH, SEQ, D_MODEL)),
        "norm_g": 1.0 + 0.02 * nrm(ks[1], (DEPTH, D_MODEL)),
        "w_in": nrm(ks[2], (DEPTH, D_MODEL, D_IN)) * D_MODEL ** -0.5,
        "w_out": nrm(ks[3], (DEPTH, D_MIX, D_MODEL)) * D_MIX ** -0.5,
        "qn_a": 1.0 + 0.02 * nrm(ks[4], (DEPTH, HEAD_DIM)),
        "kn_a": 1.0 + 0.02 * nrm(ks[5], (DEPTH, HEAD_DIM)),
        "sink_b": 0.5 * nrm(ks[6], (DEPTH, NQ_B)),
        "rpb_c": 0.1 * nrm(ks[7], (DEPTH, NH_C, 2 * NA_KH - 1, 2 * NA_KW - 1)),
        "lam_q1": 0.1 * nrm(ks[8], (DEPTH, DIFF_DIM)),
        "lam_k1": 0.1 * nrm(ks[9], (DEPTH, DIFF_DIM)),
        "lam_q2": 0.1 * nrm(ks[10], (DEPTH, DIFF_DIM)),
        "lam_k2": 0.1 * nrm(ks[11], (DEPTH, DIFF_DIM)),
        "subln_d": 1.0 + 0.02 * nrm(ks[12], (DEPTH, 2 * DIFF_DIM)),
        "final_g": 1.0 + 0.02 * nrm(ks[13], (D_MODEL,)),
    }


def reference(x, norm_g, w_in, w_out, qn_a, kn_a, sink_b, rpb_c, lam_q1, lam_k1, lam_q2, lam_k2, subln_d, final_g):
    b, s, _ = x.shape
    t = jnp.arange(s)
    row = t // GRID_W
    col = t % GRID_W
    offs = [int(o) for o in np.cumsum(_in_sizes())[:-1]]
    sc_hd = HEAD_DIM ** -0.5
    for l in range(DEPTH):
        h = _rmsnorm(x, norm_g[l])
        z = jnp.einsum('bsd,de->bse', h, w_in[l])
        (q_a, k_a, v_a, g_a, q_b, k_b, v_b, g_b,
         q_c, k_c, v_c, g_c, q_d, k_d, v_d, g_d) = jnp.split(z, offs, axis=-1)

        q_a = _rmsnorm(q_a.reshape(b, s, NKV_A, NQ_A // NKV_A, HEAD_DIM), qn_a[l])
        k_a = _rmsnorm(k_a.reshape(b, s, NKV_A, HEAD_DIM), kn_a[l])
        q_a = _axial_rope(q_a, row, col)
        k_a = _axial_rope(k_a, row, col)
        o_a = _dense_blocked(q_a, k_a, v_a.reshape(b, s, NKV_A, HEAD_DIM), sc_hd)

        q_b = _rope(q_b.reshape(b, s, NKV_B, NQ_B // NKV_B, HEAD_DIM), t)
        k_b = _rope(k_b.reshape(b, s, NKV_B, HEAD_DIM), t)
        o_b = _window_sink(q_b, k_b, v_b.reshape(b, s, NKV_B, HEAD_DIM), sink_b[l], sc_hd)

        o_c = _neighbourhood(q_c.reshape(b, s, NH_C, HEAD_DIM), k_c.reshape(b, s, NH_C, HEAD_DIM),
                             v_c.reshape(b, s, NH_C, HEAD_DIM), rpb_c[l], sc_hd)

        lam_init = 0.8 - 0.6 * math.exp(-0.3 * l)
        lam = (jnp.exp(jnp.sum(lam_q1[l].astype(jnp.float32) * lam_k1[l].astype(jnp.float32)))
               - jnp.exp(jnp.sum(lam_q2[l].astype(jnp.float32) * lam_k2[l].astype(jnp.float32)))
               + lam_init)
        q_d = _rope(q_d.reshape(b, s, NKV_D, NQ_D // NKV_D, 2, DIFF_DIM), t)
        k_d = _rope(k_d.reshape(b, s, NKV_D, 2, DIFF_DIM), t)
        o_d = _diff_blocked(q_d, k_d, v_d.reshape(b, s, NKV_D, 2 * DIFF_DIM), lam, DIFF_DIM ** -0.5)
        o_d = (_rmsnorm(o_d, subln_d[l]) * (1.0 - lam_init)).reshape(b, s, NQ_D * 2 * DIFF_DIM)

        mix = jnp.concatenate([o_a * jax.nn.silu(g_a), o_b * jax.nn.silu(g_b),
                               o_c * jax.nn.silu(g_c), o_d * jax.nn.silu(g_d)], axis=-1)
        x = x + jnp.einsum('bsm,md->bsd', mix, w_out[l])
    return _rmsnorm(x, final_g)
```

```python
import functools
import math

import jax
import jax.numpy as jnp
import numpy as np
from jax import lax
from jax.experimental import pallas as pl
from jax.experimental.pallas import tpu as pltpu

HEAD_DIM = 64
GRID_W = 64
DIFF_DIM = 32
WINDOW = 128
NA_KH = 8
NA_KW = 16
ROPE_THETA = 10000.0
EPS = 1e-6
NEG_INF = -1e30
LOG2E = math.log2(math.e)

N_HEADS = 4
BRANCH_W = N_HEADS * HEAD_DIM
KV_PAD = 128
ONES_ROWS = 16
V_EXT = HEAD_DIM + ONES_ROWS

V7X_VMEM_BYTES = 64 * 1024 * 1024
VMEM_LIMIT = V7X_VMEM_BYTES * 3 // 4

TM_PROJ = 256
TQ = 512
TK = 256
WIN_KEYS = TQ + 2 * WINDOW
NBR_ROWS = TQ // GRID_W
NBR_KEYS = (NBR_ROWS + NA_KH) * GRID_W

F32 = jnp.float32
BF16 = jnp.bfloat16


def _params(sem):
    return pltpu.CompilerParams(dimension_semantics=sem, vmem_limit_bytes=VMEM_LIMIT)


def _transpose_kernel(x_ref, o_ref):
    o_ref[...] = x_ref[...].T


def _transpose_in(x2d):
    t, d = x2d.shape
    tm = 512
    return pl.pallas_call(
        _transpose_kernel,
        grid=(t // tm,),
        in_specs=[pl.BlockSpec((tm, d), lambda i: (i, 0))],
        out_specs=pl.BlockSpec((d, tm), lambda i: (0, i)),
        out_shape=jax.ShapeDtypeStruct((d, t), F32),
        compiler_params=_params(("parallel",)),
        name="transpose_in",
    )(x2d)


def _rot_quarters(y):
    return jnp.concatenate([y[16:32], y[0:16], y[48:64], y[32:48]], axis=0)


def _rot_halves(y):
    return jnp.concatenate([y[32:64], y[0:32]], axis=0)


def _head_rms(y, g_col):
    ms = jnp.mean(y * y, axis=0, keepdims=True)
    return y * lax.rsqrt(ms + EPS) * g_col


def _pad_rows(y, slot, n_slots):
    z = jnp.zeros_like(y)
    return jnp.concatenate([y if s == slot else z for s in range(n_slots)], axis=0)


def _silu(g):
    return g * (1.0 / (1.0 + jnp.exp(-g)))


def _in_proj_kernel(xT_ref, ng_ref, w_ref, qn_ref, kn_ref,
                    ca_ref, sa_ref, cb_ref, sb_ref, cd_ref, sd_ref,
                    qa_ref, ka_ref, va_ref, qb_ref, kb_ref, vb_ref,
                    qc_ref, kc_ref, vc_ref, qd_ref, kd_ref, vd_ref, sg_ref,
                    h_ref):
    x = xT_ref[...]
    ms = jnp.mean(x * x, axis=0, keepdims=True)
    h_ref[...] = (x * lax.rsqrt(ms + EPS) * ng_ref[...]).astype(BF16)

    def proj(r0, nrows):
        return jnp.dot(w_ref[r0:r0 + nrows, :], h_ref[...], preferred_element_type=F32)

    sc_hd = HEAD_DIM ** -0.5
    ca, sa = ca_ref[...], sa_ref[...]
    cb, sb = cb_ref[...], sb_ref[...]
    cd, sd = cd_ref[...], sd_ref[...]
    qn, kn = qn_ref[...], kn_ref[...]

    def rope_a(y):
        return y * ca + _rot_quarters(y) * sa

    def rope_b(y):
        return y * cb + _rot_halves(y) * sb

    def rope_d(y):
        return y * cd + _rot_quarters(y) * sd

    base = 0
    z = proj(base, 256)
    for hh in range(N_HEADS):
        y = rope_a(_head_rms(z[64 * hh:64 * hh + 64], qn)) * (sc_hd * LOG2E)
        qa_ref[hh] = _pad_rows(y, hh // 2, 2).astype(BF16)
    z = proj(base + 256, 128)
    kk = jnp.concatenate([rope_a(_head_rms(z[64 * j:64 * j + 64], kn)) for j in range(2)], axis=0)
    ka_ref[...] = kk.T.astype(BF16)
    va_ref[...] = proj(base + 384, 128).astype(BF16)
    sg_ref[0:256, :] = _silu(proj(base + 512, 256)).astype(BF16)

    base = 768
    z = proj(base, 256)
    for hh in range(N_HEADS):
        y = rope_b(z[64 * hh:64 * hh + 64]) * sc_hd
        qb_ref[hh] = _pad_rows(y, hh // 2, 2).astype(BF16)
    z = proj(base + 256, 128)
    kk = jnp.concatenate([rope_b(z[64 * j:64 * j + 64]) for j in range(2)], axis=0)
    kb_ref[...] = kk.T.astype(BF16)
    vb_ref[...] = proj(base + 384, 128).astype(BF16)
    sg_ref[256:512, :] = _silu(proj(base + 512, 256)).astype(BF16)

    base = 1536
    z = proj(base, 256)
    for hh in range(N_HEADS):
        y = z[64 * hh:64 * hh + 64] * sc_hd
        qc_ref[hh] = _pad_rows(y, hh % 2, 2).astype(BF16)
    z = proj(base + 256, 256)
    kc_ref[...] = z.T.astype(BF16)
    vc_ref[...] = proj(base + 512, 256).astype(BF16)
    sg_ref[512:768, :] = _silu(proj(base + 768, 256)).astype(BF16)

    base = 2560
    z = proj(base, 256)
    sc_dd = DIFF_DIM ** -0.5
    for hh in range(N_HEADS):
        y = rope_d(z[64 * hh:64 * hh + 64]) * (sc_dd * LOG2E)
        for c in range(2):
            qd_ref[2 * hh + c] = _pad_rows(y[32 * c:32 * c + 32], 2 * (hh // 2) + c, 4).astype(BF16)
    z = proj(base + 256, 128)
    kk = jnp.concatenate([rope_d(z[64 * j:64 * j + 64]) for j in range(2)], axis=0)
    kd_ref[...] = kk.T.astype(BF16)
    vd_ref[...] = proj(base + 384, 128).astype(BF16)
    sg_ref[768:1024, :] = _silu(proj(base + 512, 256)).astype(BF16)


def _in_proj(xT, ng_col, w_inT, qn_col, kn_col, tables, seq):
    d, t = xT.shape
    tm = TM_PROJ
    nt_seq = seq // tm
    d_in = w_inT.shape[0]
    full = lambda shape: pl.BlockSpec(shape, lambda i: (0,) * len(shape))
    tab = pl.BlockSpec((HEAD_DIM, tm), lambda i: (0, i % nt_seq))
    qpad = lambda n: pl.BlockSpec((n, KV_PAD, tm), lambda i: (0, 0, i))
    nat = lambda w: pl.BlockSpec((tm, w), lambda i: (i, 0))
    chan = lambda w: pl.BlockSpec((w, tm), lambda i: (0, i))
    sds = jax.ShapeDtypeStruct
    out_shape = [
        sds((N_HEADS, KV_PAD, t), BF16), sds((t, 128), BF16), sds((128, t), BF16),
        sds((N_HEADS, KV_PAD, t), BF16), sds((t, 128), BF16), sds((128, t), BF16),
        sds((N_HEADS, KV_PAD, t), BF16), sds((t, 256), BF16), sds((256, t), BF16),
        sds((2 * N_HEADS, KV_PAD, t), BF16), sds((t, 128), BF16), sds((128, t), BF16),
        sds((4 * BRANCH_W, t), BF16),
    ]
    out_specs = [
        qpad(N_HEADS), nat(128), chan(128),
        qpad(N_HEADS), nat(128), chan(128),
        qpad(N_HEADS), nat(256), chan(256),
        qpad(2 * N_HEADS), nat(128), chan(128),
        chan(4 * BRANCH_W),
    ]
    return pl.pallas_call(
        _in_proj_kernel,
        grid=(t // tm,),
        in_specs=[chan(d), full((d, 1)), full((d_in, d)), full((HEAD_DIM, 1)), full((HEAD_DIM, 1)),
                  tab, tab, tab, tab, tab, tab],
        out_specs=out_specs,
        out_shape=out_shape,
        scratch_shapes=[pltpu.VMEM((d, tm), BF16)],
        compiler_params=_params(("parallel",)),
        name="in_proj",
    )(xT, ng_col, w_inT, qn_col, kn_col, *tables)


def _v_ext(v):
    return jnp.concatenate([v, jnp.ones((ONES_ROWS, v.shape[1]), v.dtype)], axis=0)


def _dense_kernel(*refs, n_q, diff, lam_init, seq):
    if diff:
        (q_ref, k_ref, v_ref, sg_ref, lam_ref, sub_ref, o_ref, m_ref, acc_ref) = refs
    else:
        (q_ref, k_ref, v_ref, sg_ref, o_ref, m_ref, acc_ref) = refs

    m_ref[...] = jnp.full(m_ref.shape, -jnp.inf, F32)
    acc_ref[...] = jnp.zeros(acc_ref.shape, F32)

    def body(j, carry):
        k0 = pl.multiple_of(j * TK, TK)
        k = k_ref[pl.ds(k0, TK), :]
        v = _v_ext(v_ref[:, pl.ds(k0, TK)])
        for i in range(n_q):
            s = jnp.dot(k, q_ref[i], preferred_element_type=F32)
            m_old = m_ref[i]
            m_new = jnp.maximum(m_old, jnp.max(s, axis=0, keepdims=True))
            alpha = jnp.exp2(m_old - m_new)
            p = jnp.exp2(s - m_new).astype(BF16)
            acc_ref[i] = alpha * acc_ref[i] + jnp.dot(v, p, preferred_element_type=F32)
            m_ref[i] = m_new
        return carry

    lax.fori_loop(0, seq // TK, body, 0)

    def normalized(i):
        acc = acc_ref[i]
        return acc[0:HEAD_DIM] / acc[HEAD_DIM:HEAD_DIM + 1]

    if diff:
        lv = lam_ref[...]
        lam = (jnp.exp(jnp.sum(lv[0:1] * lv[1:2], axis=1, keepdims=True))
               - jnp.exp(jnp.sum(lv[2:3] * lv[3:4], axis=1, keepdims=True)) + lam_init)
        o = normalized(0) - lam * normalized(1)
        ms = jnp.mean(o * o, axis=0, keepdims=True)
        o = o * lax.rsqrt(ms + EPS) * sub_ref[...] * (1.0 - lam_init)
        o_ref[...] = (o * sg_ref[...].astype(F32)).astype(BF16)
    else:
        for i in range(n_q):
            r = slice(HEAD_DIM * i, HEAD_DIM * (i + 1))
            o_ref[r, :] = (normalized(i) * sg_ref[r, :].astype(F32)).astype(BF16)


def _dense_attn(q_pad, k_nat, vT, sgT, sg_row0, batch, seq, diff, lam_vecs=None, subln_col=None,
                lam_init=0.0):
    nt = seq // TQ
    n_q = 2
    out_rows = HEAD_DIM if diff else 2 * HEAD_DIM
    n_groups = BRANCH_W // out_rows
    sg_blk0 = sg_row0 // out_rows
    in_specs = [
        pl.BlockSpec((n_q, KV_PAD, TQ), lambda b, g, i: (g, 0, b * nt + i)),
        pl.BlockSpec((seq, KV_PAD), lambda b, g, i: (b, 0)),
        pl.BlockSpec((HEAD_DIM, seq), (lambda b, g, i: (g // 2, b)) if diff else (lambda b, g, i: (g, b))),
        pl.BlockSpec((out_rows, TQ), lambda b, g, i: (sg_blk0 + g, b * nt + i)),
    ]
    args = [q_pad, k_nat, vT, sgT]
    if diff:
        in_specs += [pl.BlockSpec((4, DIFF_DIM), lambda b, g, i: (0, 0)),
                     pl.BlockSpec((HEAD_DIM, 1), lambda b, g, i: (0, 0))]
        args += [lam_vecs, subln_col]
    return pl.pallas_call(
        functools.partial(_dense_kernel, n_q=n_q, diff=diff, lam_init=lam_init, seq=seq),
        grid=(batch, n_groups, nt),
        in_specs=in_specs,
        out_specs=pl.BlockSpec((out_rows, TQ), lambda b, g, i: (g, b * nt + i)),
        out_shape=jax.ShapeDtypeStruct((BRANCH_W, batch * seq), BF16),
        scratch_shapes=[pltpu.VMEM((n_q, 1, TQ), F32), pltpu.VMEM((n_q, V_EXT, TQ), F32)],
        compiler_params=_params(("parallel", "parallel", "parallel")),
        name="diff_attn" if diff else "dense_attn",
    )(*args)


def _window_kernel(sink_ref, q_ref, k_ref, v_ref, sg_ref, o_ref, *, seq):
    h = pl.program_id(1)
    q0 = pl.program_id(2) * TQ
    start = pl.multiple_of(jnp.clip(q0 - WINDOW, 0, seq - WIN_KEYS), WINDOW)
    k = k_ref[pl.ds(start, WIN_KEYS), :]
    v = _v_ext(v_ref[:, pl.ds(start, WIN_KEYS)])
    s = jnp.dot(k, q_ref[0], preferred_element_type=F32)
    kpos = start + lax.broadcasted_iota(jnp.int32, s.shape, 0)
    qpos = q0 + lax.broadcasted_iota(jnp.int32, s.shape, 1)
    s = jnp.where(jnp.abs(qpos - kpos) <= WINDOW, s, NEG_INF)
    sink = sink_ref[h]
    m = jnp.maximum(jnp.max(s, axis=0, keepdims=True), sink)
    p = jnp.exp(s - m).astype(BF16)
    acc = jnp.dot(v, p, preferred_element_type=F32)
    denom = acc[HEAD_DIM:HEAD_DIM + 1] + jnp.exp(sink - m)
    o_ref[...] = (acc[0:HEAD_DIM] / denom * sg_ref[...].astype(F32)).astype(BF16)


def _window_attn(sink, q_pad, k_nat, vT, sgT, sg_row0, batch, seq):
    nt = seq // TQ
    sg_blk0 = sg_row0 // HEAD_DIM
    return pl.pallas_call(
        functools.partial(_window_kernel, seq=seq),
        grid=(batch, N_HEADS, nt),
        in_specs=[
            pl.BlockSpec(memory_space=pltpu.SMEM),
            pl.BlockSpec((1, KV_PAD, TQ), lambda b, h, i: (h, 0, b * nt + i)),
            pl.BlockSpec((seq, KV_PAD), lambda b, h, i: (b, 0)),
            pl.BlockSpec((HEAD_DIM, seq), lambda b, h, i: (h // 2, b)),
            pl.BlockSpec((HEAD_DIM, TQ), lambda b, h, i: (sg_blk0 + h, b * nt + i)),
        ],
        out_specs=pl.BlockSpec((HEAD_DIM, TQ), lambda b, h, i: (h, b * nt + i)),
        out_shape=jax.ShapeDtypeStruct((BRANCH_W, batch * seq), BF16),
        compiler_params=_params(("parallel", "parallel", "parallel")),
        name="window_attn",
    )(sink, q_pad, k_nat, vT, sgT)


def _nbr_start(tile, seq):
    return jnp.clip(tile * TQ - (NA_KH // 2) * GRID_W, 0, seq - NBR_KEYS)


def _nbr_kernel(q_ref, k_ref, v_ref, bias_ref, sg_ref, o_ref, *, seq):
    start = pl.multiple_of(_nbr_start(pl.program_id(2), seq), GRID_W * (NA_KH // 2))
    k = k_ref[pl.ds(start, NBR_KEYS), :]
    v = _v_ext(v_ref[:, pl.ds(start, NBR_KEYS)])
    s = jnp.dot(k, q_ref[0], preferred_element_type=F32) + bias_ref[0, 0]
    m = jnp.max(s, axis=0, keepdims=True)
    p = jnp.exp(s - m).astype(BF16)
    acc = jnp.dot(v, p, preferred_element_type=F32)
    o_ref[...] = (acc[0:HEAD_DIM] / acc[HEAD_DIM:HEAD_DIM + 1] * sg_ref[...].astype(F32)).astype(BF16)


def _nbr_bias_tables(rpb, seq):
    nt = seq // TQ
    rows = seq // GRID_W
    kh = min(NA_KH, rows)
    tabs = []
    for tile in (0, 1, nt - 1):
        start = int(np.clip(tile * TQ - (NA_KH // 2) * GRID_W, 0, seq - NBR_KEYS))
        ktok = start + np.arange(NBR_KEYS)[:, None]
        qtok = tile * TQ + np.arange(TQ)[None, :]
        krow, kcol = ktok // GRID_W, ktok % GRID_W
        qrow, qcol = qtok // GRID_W, qtok % GRID_W
        rs = np.clip(qrow - kh // 2, 0, rows - kh)
        cs = np.clip(qcol - NA_KW // 2, 0, GRID_W - NA_KW)
        ok = (krow >= rs) & (krow < rs + kh) & (kcol >= cs) & (kcol < cs + NA_KW)
        dr = np.clip(krow - qrow + NA_KH - 1, 0, 2 * NA_KH - 2)
        dc = np.clip(kcol - qcol, -(NA_KW - 1), NA_KW - 1) + NA_KW - 1
        dr = np.broadcast_to(dr, ok.shape)
        gathered = rpb.astype(F32)[:, dr, dc]
        tabs.append(jnp.where(ok[None], gathered, NEG_INF))
    return jnp.stack(tabs)


def _nbr_attn(q_pad, k_nat, vT, bias, sgT, sg_row0, batch, seq):
    nt = seq // TQ
    sg_blk0 = sg_row0 // HEAD_DIM

    def variant(i):
        return jnp.where(i == 0, 0, jnp.where(i == nt - 1, 2, 1))

    return pl.pallas_call(
        functools.partial(_nbr_kernel, seq=seq),
        grid=(batch, N_HEADS, nt),
        in_specs=[
            pl.BlockSpec((1, KV_PAD, TQ), lambda b, h, i: (h, 0, b * nt + i)),
            pl.BlockSpec((seq, KV_PAD), lambda b, h, i: (b, h // 2)),
            pl.BlockSpec((HEAD_DIM, seq), lambda b, h, i: (h, b)),
            pl.BlockSpec((1, 1, NBR_KEYS, TQ), lambda b, h, i: (variant(i), h, 0, 0)),
            pl.BlockSpec((HEAD_DIM, TQ), lambda b, h, i: (sg_blk0 + h, b * nt + i)),
        ],
        out_specs=pl.BlockSpec((HEAD_DIM, TQ), lambda b, h, i: (h, b * nt + i)),
        out_shape=jax.ShapeDtypeStruct((BRANCH_W, batch * seq), BF16),
        compiler_params=_params(("parallel", "parallel", "parallel")),
        name="nbr_attn",
    )(q_pad, k_nat, vT, bias, sgT)


def _out_proj_kernel(xT_ref, ma_ref, mb_ref, mc_ref, md_ref, w_ref, *rest, final):
    acc = xT_ref[...]
    for j, m_ref in enumerate((ma_ref, mb_ref, mc_ref, md_ref)):
        acc = acc + jnp.dot(w_ref[:, BRANCH_W * j:BRANCH_W * (j + 1)], m_ref[...],
                            preferred_element_type=F32)
    if final:
        fg_ref, o_ref = rest
        ms = jnp.mean(acc * acc, axis=0, keepdims=True)
        o_ref[...] = (acc * lax.rsqrt(ms + EPS) * fg_ref[...]).T
    else:
        (o_ref,) = rest
        o_ref[...] = acc


def _out_proj(xT, mixes, w_outT, final_g_col=None):
    d, t = xT.shape
    tm = TM_PROJ
    final = final_g_col is not None
    chan = lambda w: pl.BlockSpec((w, tm), lambda i: (0, i))
    in_specs = [chan(d)] + [chan(BRANCH_W)] * 4 + [pl.BlockSpec((d, 4 * BRANCH_W), lambda i: (0, 0))]
    args = [xT, *mixes, w_outT]
    if final:
        in_specs.append(pl.BlockSpec((d, 1), lambda i: (0, 0)))
        args.append(final_g_col)
        out_spec = pl.BlockSpec((tm, d), lambda i: (i, 0))
        out_shape = jax.ShapeDtypeStruct((t, d), F32)
    else:
        out_spec = chan(d)
        out_shape = jax.ShapeDtypeStruct((d, t), F32)
    return pl.pallas_call(
        functools.partial(_out_proj_kernel, final=final),
        grid=(t // tm,),
        in_specs=in_specs,
        out_specs=out_spec,
        out_shape=out_shape,
        compiler_params=_params(("parallel",)),
        name="out_proj_final" if final else "out_proj",
    )(*args)


def _rope_tables(seq):
    t = jnp.arange(seq)
    row, col = t // GRID_W, t % GRID_W

    def cs(pos, d):
        inv = jnp.power(ROPE_THETA, -jnp.arange(0, d, 2, dtype=F32) / d)
        ang = pos.astype(F32)[None, :] * inv[:, None]
        return jnp.cos(ang), jnp.sin(ang)

    cr, sr = cs(row, HEAD_DIM // 2)
    cc, sc = cs(col, HEAD_DIM // 2)
    ct, st = cs(t, HEAD_DIM)
    cdt, sdt = cs(t, DIFF_DIM)
    cat = lambda *a: jnp.concatenate(a, axis=0)
    return (cat(cr, cr, cc, cc), cat(-sr, sr, -sc, sc),
            cat(ct, ct), cat(-st, st),
            cat(cdt, cdt, cdt, cdt), cat(-sdt, sdt, -sdt, sdt))


def kernel(x, norm_g, w_in, w_out, qn_a, kn_a, sink_b, rpb_c, lam_q1, lam_k1, lam_q2, lam_k2, subln_d, final_g):
    batch, seq, d = x.shape
    depth = w_in.shape[0]
    tables = _rope_tables(seq)
    xT = _transpose_in(x.reshape(batch * seq, d))
    out = None
    for l in range(depth):
        w_inT = w_in[l].T.astype(BF16)
        w_outT = w_out[l].T.astype(BF16)
        (qa, ka, va, qb, kb, vb, qc, kc, vc, qd, kd, vd, sg) = _in_proj(
            xT, norm_g[l].reshape(d, 1), w_inT, qn_a[l].reshape(HEAD_DIM, 1), kn_a[l].reshape(HEAD_DIM, 1),
            tables, seq)
        mix_a = _dense_attn(qa, ka, va, sg, 0, batch, seq, diff=False)
        mix_b = _window_attn(sink_b[l], qb, kb, vb, sg, BRANCH_W, batch, seq)
        mix_c = _nbr_attn(qc, kc, vc, _nbr_bias_tables(rpb_c[l], seq), sg, 2 * BRANCH_W, batch, seq)
        lam_init = 0.8 - 0.6 * math.exp(-0.3 * l)
        lam_vecs = jnp.stack([lam_q1[l], lam_k1[l], lam_q2[l], lam_k2[l]]).astype(F32)
        mix_d = _dense_attn(qd, kd, vd, sg, 3 * BRANCH_W, batch, seq, diff=True, lam_vecs=lam_vecs,
                            subln_col=subln_d[l].reshape(HEAD_DIM, 1), lam_init=lam_init)
        mixes = (mix_a, mix_b, mix_c, mix_d)
        if l == depth - 1:
            out = _out_proj(xT, mixes, w_outT, final_g.reshape(d, 1))
        else:
            xT = _out_proj(xT, mixes, w_outT)
    return out.reshape(batch, seq, d)
```

```python
import functools
import math

import jax
import jax.numpy as jnp
import numpy as np
from jax import lax
from jax.experimental import pallas as pl
from jax.experimental.pallas import tpu as pltpu

HEAD_DIM = 64
GRID_W = 64
DIFF_DIM = 32
WINDOW = 128
NA_KH = 8
NA_KW = 16
ROPE_THETA = 10000.0
EPS = 1e-6
NEG_INF = -1e30
LOG2E = math.log2(math.e)

N_HEADS = 4
BRANCH_W = N_HEADS * HEAD_DIM
KV_PAD = 128
ONES_ROWS = 16
V_EXT = HEAD_DIM + ONES_ROWS

V7X_VMEM_BYTES = 64 * 1024 * 1024
VMEM_LIMIT = V7X_VMEM_BYTES * 3 // 4

TM_PROJ = 256
TQ = 512
TK = 256
WIN_KEYS = TQ + 2 * WINDOW
NBR_ROWS = TQ // GRID_W
NBR_KEYS = (NBR_ROWS + NA_KH) * GRID_W

F32 = jnp.float32
BF16 = jnp.bfloat16


def _params(sem):
    return pltpu.CompilerParams(dimension_semantics=sem, vmem_limit_bytes=VMEM_LIMIT)


def _transpose_kernel(x_ref, o_ref):
    o_ref[...] = x_ref[...].T


def _transpose_in(x2d):
    t, d = x2d.shape
    tm = 512
    return pl.pallas_call(
        _transpose_kernel,
        grid=(t // tm,),
        in_specs=[pl.BlockSpec((tm, d), lambda i: (i, 0))],
        out_specs=pl.BlockSpec((d, tm), lambda i: (0, i)),
        out_shape=jax.ShapeDtypeStruct((d, t), F32),
        compiler_params=_params(("parallel",)),
        name="transpose_in",
    )(x2d)


def _rot_quarters(y):
    return jnp.concatenate([y[16:32], y[0:16], y[48:64], y[32:48]], axis=0)


def _rot_halves(y):
    return jnp.concatenate([y[32:64], y[0:32]], axis=0)


def _head_rms(y, g_col):
    ms = jnp.mean(y * y, axis=0, keepdims=True)
    return y * lax.rsqrt(ms + EPS) * g_col


def _pad_rows(y, slot, n_slots):
    z = jnp.zeros_like(y)
    return jnp.concatenate([y if s == slot else z for s in range(n_slots)], axis=0)


def _silu(g):
    return g * (1.0 / (1.0 + jnp.exp(-g)))


def _in_proj_kernel(xT_ref, ng_ref, w_ref, qn_ref, kn_ref,
                    ca_ref, sa_ref, cb_ref, sb_ref, cd_ref, sd_ref,
                    qa_ref, ka_ref, va_ref, qb_ref, kb_ref, vb_ref,
                    qc_ref, kc_ref, vc_ref, qd_ref, kd_ref, vd_ref, sg_ref,
                    h_ref):
    x = xT_ref[...]
    ms = jnp.mean(x * x, axis=0, keepdims=True)
    h_ref[...] = (x * lax.rsqrt(ms + EPS) * ng_ref[...]).astype(BF16)

    def proj(r0, nrows):
        return jnp.dot(w_ref[r0:r0 + nrows, :], h_ref[...], preferred_element_type=F32)

    sc_hd = HEAD_DIM ** -0.5
    ca, sa = ca_ref[...], sa_ref[...]
    cb, sb = cb_ref[...], sb_ref[...]
    cd, sd = cd_ref[...], sd_ref[...]
    qn, kn = qn_ref[...], kn_ref[...]

    def rope_a(y):
        return y * ca + _rot_quarters(y) * sa

    def rope_b(y):
        return y * cb + _rot_halves(y) * sb

    def rope_d(y):
        return y * cd + _rot_quarters(y) * sd

    base = 0
    z = proj(base, 256)
    for hh in range(N_HEADS):
        y = rope_a(_head_rms(z[64 * hh:64 * hh + 64], qn)) * (sc_hd * LOG2E)
        qa_ref[hh] = _pad_rows(y, hh // 2, 2).astype(BF16)
    z = proj(base + 256, 128)
    kk = jnp.concatenate([rope_a(_head_rms(z[64 * j:64 * j + 64], kn)) for j in range(2)], axis=0)
    ka_ref[...] = kk.T.astype(BF16)
    va_ref[...] = proj(base + 384, 128).astype(BF16)
    sg_ref[0:256, :] = _silu(proj(base + 512, 256)).astype(BF16)

    base = 768
    z = proj(base, 256)
    for hh in range(N_HEADS):
        y = rope_b(z[64 * hh:64 * hh + 64]) * sc_hd
        qb_ref[hh] = _pad_rows(y, hh // 2, 2).astype(BF16)
    z = proj(base + 256, 128)
    kk = jnp.concatenate([rope_b(z[64 * j:64 * j + 64]) for j in range(2)], axis=0)
    kb_ref[...] = kk.T.astype(BF16)
    vb_ref[...] = proj(base + 384, 128).astype(BF16)
    sg_ref[256:512, :] = _silu(proj(base + 512, 256)).astype(BF16)

    base = 1536
    z = proj(base, 256)
    for hh in range(N_HEADS):
        y = z[64 * hh:64 * hh + 64] * sc_hd
        qc_ref[hh] = _pad_rows(y, hh % 2, 2).astype(BF16)
    z = proj(base + 256, 256)
    kc_ref[...] = z.T.astype(BF16)
    vc_ref[...] = proj(base + 512, 256).astype(BF16)
    sg_ref[512:768, :] = _silu(proj(base + 768, 256)).astype(BF16)

    base = 2560
    z = proj(base, 256)
    sc_dd = DIFF_DIM ** -0.5
    for hh in range(N_HEADS):
        y = rope_d(z[64 * hh:64 * hh + 64]) * (sc_dd * LOG2E)
        for c in range(2):
            qd_ref[2 * hh + c] = _pad_rows(y[32 * c:32 * c + 32], 2 * (hh // 2) + c, 4).astype(BF16)
    z = proj(base + 256, 128)
    kk = jnp.concatenate([rope_d(z[64 * j:64 * j + 64]) for j in range(2)], axis=0)
    kd_ref[...] = kk.T.astype(BF16)
    vd_ref[...] = proj(base + 384, 128).astype(BF16)
    sg_ref[768:1024, :] = _silu(proj(base + 512, 256)).astype(BF16)


def _in_proj(xT, ng_col, w_inT, qn_col, kn_col, tables, seq):
    d, t = xT.shape
    tm = TM_PROJ
    nt_seq = seq // tm
    d_in = w_inT.shape[0]
    full = lambda shape: pl.BlockSpec(shape, lambda i: (0,) * len(shape))
    tab = pl.BlockSpec((HEAD_DIM, tm), lambda i: (0, i % nt_seq))
    qpad = lambda n: pl.BlockSpec((n, KV_PAD, tm), lambda i: (0, 0, i))
    nat = lambda w: pl.BlockSpec((tm, w), lambda i: (i, 0))
    chan = lambda w: pl.BlockSpec((w, tm), lambda i: (0, i))
    sds = jax.ShapeDtypeStruct
    out_shape = [
        sds((N_HEADS, KV_PAD, t), BF16), sds((t, 128), BF16), sds((128, t), BF16),
        sds((N_HEADS, KV_PAD, t), BF16), sds((t, 128), BF16), sds((128, t), BF16),
        sds((N_HEADS, KV_PAD, t), BF16), sds((t, 256), BF16), sds((256, t), BF16),
        sds((2 * N_HEADS, KV_PAD, t), BF16), sds((t, 128), BF16), sds((128, t), BF16),
        sds((4 * BRANCH_W, t), BF16),
    ]
    out_specs = [
        qpad(N_HEADS), nat(128), chan(128),
        qpad(N_HEADS), nat(128), chan(128),
        qpad(N_HEADS), nat(256), chan(256),
        qpad(2 * N_HEADS), nat(128), chan(128),
        chan(4 * BRANCH_W),
    ]
    return pl.pallas_call(
        _in_proj_kernel,
        grid=(t // tm,),
        in_specs=[chan(d), full((d, 1)), full((d_in, d)), full((HEAD_DIM, 1)), full((HEAD_DIM, 1)),
                  tab, tab, tab, tab, tab, tab],
        out_specs=out_specs,
        out_shape=out_shape,
        scratch_shapes=[pltpu.VMEM((d, tm), BF16)],
        compiler_params=_params(("parallel",)),
        name="in_proj",
    )(xT, ng_col, w_inT, qn_col, kn_col, *tables)


def _v_ext(v):
    return jnp.concatenate([v, jnp.ones((ONES_ROWS, v.shape[1]), v.dtype)], axis=0)


def _dense_kernel(*refs, n_q, diff, lam_init, seq):
    if diff:
        (q_ref, k_ref, v_ref, sg_ref, lam_ref, sub_ref, o_ref, s_ref, acc_ref) = refs
    else:
        (q_ref, k_ref, v_ref, sg_ref, o_ref, s_ref, acc_ref) = refs
    n_chunks = seq // TK

    acc_ref[...] = jnp.zeros(acc_ref.shape, F32)

    def score_stage(slot, chunk):
        k0 = pl.multiple_of(chunk * TK, TK)
        k = k_ref[pl.ds(k0, TK), :]
        maxima = []
        for i in range(n_q):
            s = jnp.dot(k, q_ref[i], preferred_element_type=F32)
            s_ref[slot, i] = s
            maxima.append(jnp.max(s, axis=0, keepdims=True))
        return maxima

    def value_stage(slot, chunk, m_old, m_new):
        k0 = pl.multiple_of(chunk * TK, TK)
        v = _v_ext(v_ref[:, pl.ds(k0, TK)])
        for i in range(n_q):
            alpha = jnp.exp2(m_old[i] - m_new[i])
            p = jnp.exp2(s_ref[slot, i] - m_new[i]).astype(BF16)
            acc_ref[i] = alpha * acc_ref[i] + jnp.dot(v, p, preferred_element_type=F32)

    def body(jj, carry):
        m_old, m_run = carry
        c0 = 2 * jj
        nxt = score_stage(1, c0 + 1)
        value_stage(0, c0, m_old, m_run)
        m_old, m_run = m_run, [jnp.maximum(a, b) for a, b in zip(m_run, nxt)]
        nxt = score_stage(0, jnp.minimum(c0 + 2, n_chunks - 1))
        value_stage(1, c0 + 1, m_old, m_run)
        m_old, m_run = m_run, [jnp.maximum(a, b) for a, b in zip(m_run, nxt)]
        return m_old, m_run

    first = score_stage(0, 0)
    minus_inf = [jnp.full((1, TQ), -jnp.inf, F32)] * n_q
    lax.fori_loop(0, n_chunks // 2, body, (minus_inf, first))

    def normalized(i):
        acc = acc_ref[i]
        return acc[0:HEAD_DIM] / acc[HEAD_DIM:HEAD_DIM + 1]

    if diff:
        lv = lam_ref[...]
        lam = (jnp.exp(jnp.sum(lv[0:1] * lv[1:2], axis=1, keepdims=True))
               - jnp.exp(jnp.sum(lv[2:3] * lv[3:4], axis=1, keepdims=True)) + lam_init)
        o = normalized(0) - lam * normalized(1)
        ms = jnp.mean(o * o, axis=0, keepdims=True)
        o = o * lax.rsqrt(ms + EPS) * sub_ref[...] * (1.0 - lam_init)
        o_ref[...] = (o * sg_ref[...].astype(F32)).astype(BF16)
    else:
        for i in range(n_q):
            r = slice(HEAD_DIM * i, HEAD_DIM * (i + 1))
            o_ref[r, :] = (normalized(i) * sg_ref[r, :].astype(F32)).astype(BF16)


def _dense_attn(q_pad, k_nat, vT, sgT, sg_row0, batch, seq, diff, lam_vecs=None, subln_col=None,
                lam_init=0.0):
    nt = seq // TQ
    n_q = 2
    out_rows = HEAD_DIM if diff else 2 * HEAD_DIM
    n_groups = BRANCH_W // out_rows
    sg_blk0 = sg_row0 // out_rows
    in_specs = [
        pl.BlockSpec((n_q, KV_PAD, TQ), lambda b, g, i: (g, 0, b * nt + i)),
        pl.BlockSpec((seq, KV_PAD), lambda b, g, i: (b, 0)),
        pl.BlockSpec((HEAD_DIM, seq), (lambda b, g, i: (g // 2, b)) if diff else (lambda b, g, i: (g, b))),
        pl.BlockSpec((out_rows, TQ), lambda b, g, i: (sg_blk0 + g, b * nt + i)),
    ]
    args = [q_pad, k_nat, vT, sgT]
    if diff:
        in_specs += [pl.BlockSpec((4, DIFF_DIM), lambda b, g, i: (0, 0)),
                     pl.BlockSpec((HEAD_DIM, 1), lambda b, g, i: (0, 0))]
        args += [lam_vecs, subln_col]
    return pl.pallas_call(
        functools.partial(_dense_kernel, n_q=n_q, diff=diff, lam_init=lam_init, seq=seq),
        grid=(batch, n_groups, nt),
        in_specs=in_specs,
        out_specs=pl.BlockSpec((out_rows, TQ), lambda b, g, i: (g, b * nt + i)),
        out_shape=jax.ShapeDtypeStruct((BRANCH_W, batch * seq), BF16),
        scratch_shapes=[pltpu.VMEM((2, n_q, TK, TQ), F32), pltpu.VMEM((n_q, V_EXT, TQ), F32)],
        compiler_params=_params(("parallel", "parallel", "parallel")),
        name="diff_attn" if diff else "dense_attn",
    )(*args)


def _window_kernel(sink_ref, q_ref, k_ref, v_ref, sg_ref, o_ref, *, seq):
    h = pl.program_id(1)
    q0 = pl.program_id(2) * TQ
    start = pl.multiple_of(jnp.clip(q0 - WINDOW, 0, seq - WIN_KEYS), WINDOW)
    k = k_ref[pl.ds(start, WIN_KEYS), :]
    v = _v_ext(v_ref[:, pl.ds(start, WIN_KEYS)])
    s = jnp.dot(k, q_ref[0], preferred_element_type=F32)
    kpos = start + lax.broadcasted_iota(jnp.int32, s.shape, 0)
    qpos = q0 + lax.broadcasted_iota(jnp.int32, s.shape, 1)
    s = jnp.where(jnp.abs(qpos - kpos) <= WINDOW, s, NEG_INF)
    sink = sink_ref[h]
    m = jnp.maximum(jnp.max(s, axis=0, keepdims=True), sink)
    p = jnp.exp(s - m).astype(BF16)
    acc = jnp.dot(v, p, preferred_element_type=F32)
    denom = acc[HEAD_DIM:HEAD_DIM + 1] + jnp.exp(sink - m)
    o_ref[...] = (acc[0:HEAD_DIM] / denom * sg_ref[...].astype(F32)).astype(BF16)


def _window_attn(sink, q_pad, k_nat, vT, sgT, sg_row0, batch, seq):
    nt = seq // TQ
    sg_blk0 = sg_row0 // HEAD_DIM
    return pl.pallas_call(
        functools.partial(_window_kernel, seq=seq),
        grid=(batch, N_HEADS, nt),
        in_specs=[
            pl.BlockSpec(memory_space=pltpu.SMEM),
            pl.BlockSpec((1, KV_PAD, TQ), lambda b, h, i: (h, 0, b * nt + i)),
            pl.BlockSpec((seq, KV_PAD), lambda b, h, i: (b, 0)),
            pl.BlockSpec((HEAD_DIM, seq), lambda b, h, i: (h // 2, b)),
            pl.BlockSpec((HEAD_DIM, TQ), lambda b, h, i: (sg_blk0 + h, b * nt + i)),
        ],
        out_specs=pl.BlockSpec((HEAD_DIM, TQ), lambda b, h, i: (h, b * nt + i)),
        out_shape=jax.ShapeDtypeStruct((BRANCH_W, batch * seq), BF16),
        compiler_params=_params(("parallel", "parallel", "parallel")),
        name="window_attn",
    )(sink, q_pad, k_nat, vT, sgT)


def _nbr_start(tile, seq):
    return jnp.clip(tile * TQ - (NA_KH // 2) * GRID_W, 0, seq - NBR_KEYS)


def _nbr_kernel(q_ref, k_ref, v_ref, bias_ref, sg_ref, o_ref, *, seq):
    start = pl.multiple_of(_nbr_start(pl.program_id(2), seq), GRID_W * (NA_KH // 2))
    k = k_ref[pl.ds(start, NBR_KEYS), :]
    v = _v_ext(v_ref[:, pl.ds(start, NBR_KEYS)])
    s = jnp.dot(k, q_ref[0], preferred_element_type=F32) + bias_ref[0, 0]
    m = jnp.max(s, axis=0, keepdims=True)
    p = jnp.exp(s - m).astype(BF16)
    acc = jnp.dot(v, p, preferred_element_type=F32)
    o_ref[...] = (acc[0:HEAD_DIM] / acc[HEAD_DIM:HEAD_DIM + 1] * sg_ref[...].astype(F32)).astype(BF16)


N_DR = 2 * NA_KH - 1
N_DC = 2 * NA_KW - 1


def _nbr_bias_kernel(rpb_ref, o_ref, *, seq):
    h = pl.program_id(0)
    nt = seq // TQ
    rows = seq // GRID_W
    kh = min(NA_KH, rows)
    shape = (GRID_W, 2 * GRID_W)
    kc = lax.broadcasted_iota(jnp.int32, shape, 0)
    lane = lax.broadcasted_iota(jnp.int32, shape, 1)
    qc = lane % GRID_W
    dcol = jnp.clip(kc - qc, -(NA_KW - 1), NA_KW - 1) + NA_KW - 1
    cs = jnp.clip(qc - NA_KW // 2, 0, GRID_W - NA_KW)
    col_ok = (kc >= cs) & (kc < cs + NA_KW)
    hits = [dcol == j for j in range(N_DC)]
    toeplitz = []
    for dr in range(N_DR):
        t = jnp.zeros(shape, F32)
        for j in range(N_DC):
            t = jnp.where(hits[j], rpb_ref[(h * N_DR + dr) * N_DC + j], t)
        toeplitz.append(jnp.where(col_ok, t, NEG_INF))
    masked = jnp.full(shape, NEG_INF, F32)
    left = lane < GRID_W
    for vi, tile in enumerate((0, 1, nt - 1)):
        start_row = int(np.clip(tile * NBR_ROWS - NA_KH // 2, 0, rows - NBR_KEYS // GRID_W))
        for kr in range(NBR_KEYS // GRID_W):
            krow = start_row + kr
            for qp in range(NBR_ROWS // 2):
                halves = []
                for qrow in (tile * NBR_ROWS + 2 * qp, tile * NBR_ROWS + 2 * qp + 1):
                    rs = int(np.clip(qrow - kh // 2, 0, rows - kh))
                    halves.append(toeplitz[krow - qrow + NA_KH - 1] if rs <= krow < rs + kh else masked)
                blk = halves[0] if halves[0] is halves[1] else jnp.where(left, halves[0], halves[1])
                o_ref[vi, 0, GRID_W * kr:GRID_W * (kr + 1), 2 * GRID_W * qp:2 * GRID_W * (qp + 1)] = blk


def _nbr_bias_tables(rpb, seq):
    return pl.pallas_call(
        functools.partial(_nbr_bias_kernel, seq=seq),
        grid=(N_HEADS,),
        in_specs=[pl.BlockSpec(memory_space=pltpu.SMEM)],
        out_specs=pl.BlockSpec((3, 1, NBR_KEYS, TQ), lambda h: (0, h, 0, 0)),
        out_shape=jax.ShapeDtypeStruct((3, N_HEADS, NBR_KEYS, TQ), F32),
        compiler_params=_params(("parallel",)),
        name="nbr_bias",
    )(rpb.astype(F32).reshape(-1))


def _nbr_attn(q_pad, k_nat, vT, bias, sgT, sg_row0, batch, seq):
    nt = seq // TQ
    sg_blk0 = sg_row0 // HEAD_DIM

    def variant(i):
        return jnp.where(i == 0, 0, jnp.where(i == nt - 1, 2, 1))

    return pl.pallas_call(
        functools.partial(_nbr_kernel, seq=seq),
        grid=(batch, N_HEADS, nt),
        in_specs=[
            pl.BlockSpec((1, KV_PAD, TQ), lambda b, h, i: (h, 0, b * nt + i)),
            pl.BlockSpec((seq, KV_PAD), lambda b, h, i: (b, h // 2)),
            pl.BlockSpec((HEAD_DIM, seq), lambda b, h, i: (h, b)),
            pl.BlockSpec((1, 1, NBR_KEYS, TQ), lambda b, h, i: (variant(i), h, 0, 0)),
            pl.BlockSpec((HEAD_DIM, TQ), lambda b, h, i: (sg_blk0 + h, b * nt + i)),
        ],
        out_specs=pl.BlockSpec((HEAD_DIM, TQ), lambda b, h, i: (h, b * nt + i)),
        out_shape=jax.ShapeDtypeStruct((BRANCH_W, batch * seq), BF16),
        compiler_params=_params(("parallel", "parallel", "parallel")),
        name="nbr_attn",
    )(q_pad, k_nat, vT, bias, sgT)


def _out_proj_kernel(xT_ref, ma_ref, mb_ref, mc_ref, md_ref, w_ref, *rest, final):
    acc = xT_ref[...]
    for j, m_ref in enumerate((ma_ref, mb_ref, mc_ref, md_ref)):
        acc = acc + jnp.dot(w_ref[:, BRANCH_W * j:BRANCH_W * (j + 1)], m_ref[...],
                            preferred_element_type=F32)
    if final:
        fg_ref, o_ref = rest
        ms = jnp.mean(acc * acc, axis=0, keepdims=True)
        o_ref[...] = (acc * lax.rsqrt(ms + EPS) * fg_ref[...]).T
    else:
        (o_ref,) = rest
        o_ref[...] = acc


def _out_proj(xT, mixes, w_outT, final_g_col=None):
    d, t = xT.shape
    tm = TM_PROJ
    final = final_g_col is not None
    chan = lambda w: pl.BlockSpec((w, tm), lambda i: (0, i))
    in_specs = [chan(d)] + [chan(BRANCH_W)] * 4 + [pl.BlockSpec((d, 4 * BRANCH_W), lambda i: (0, 0))]
    args = [xT, *mixes, w_outT]
    if final:
        in_specs.append(pl.BlockSpec((d, 1), lambda i: (0, 0)))
        args.append(final_g_col)
        out_spec = pl.BlockSpec((tm, d), lambda i: (i, 0))
        out_shape = jax.ShapeDtypeStruct((t, d), F32)
    else:
        out_spec = chan(d)
        out_shape = jax.ShapeDtypeStruct((d, t), F32)
    return pl.pallas_call(
        functools.partial(_out_proj_kernel, final=final),
        grid=(t // tm,),
        in_specs=in_specs,
        out_specs=out_spec,
        out_shape=out_shape,
        compiler_params=_params(("parallel",)),
        name="out_proj_final" if final else "out_proj",
    )(*args)


def _rope_tables(seq):
    t = jnp.arange(seq)
    row, col = t // GRID_W, t % GRID_W

    def cs(pos, d):
        inv = jnp.power(ROPE_THETA, -jnp.arange(0, d, 2, dtype=F32) / d)
        ang = pos.astype(F32)[None, :] * inv[:, None]
        return jnp.cos(ang), jnp.sin(ang)

    cr, sr = cs(row, HEAD_DIM // 2)
    cc, sc = cs(col, HEAD_DIM // 2)
    ct, st = cs(t, HEAD_DIM)
    cdt, sdt = cs(t, DIFF_DIM)
    cat = lambda *a: jnp.concatenate(a, axis=0)
    return (cat(cr, cr, cc, cc), cat(-sr, sr, -sc, sc),
            cat(ct, ct), cat(-st, st),
            cat(cdt, cdt, cdt, cdt), cat(-sdt, sdt, -sdt, sdt))


def kernel(x, norm_g, w_in, w_out, qn_a, kn_a, sink_b, rpb_c, lam_q1, lam_k1, lam_q2, lam_k2, subln_d, final_g):
    batch, seq, d = x.shape
    depth = w_in.shape[0]
    tables = _rope_tables(seq)
    xT = _transpose_in(x.reshape(batch * seq, d))
    out = None
    for l in range(depth):
        w_inT = w_in[l].T.astype(BF16)
        w_outT = w_out[l].T.astype(BF16)
        (qa, ka, va, qb, kb, vb, qc, kc, vc, qd, kd, vd, sg) = _in_proj(
            xT, norm_g[l].reshape(d, 1), w_inT, qn_a[l].reshape(HEAD_DIM, 1), kn_a[l].reshape(HEAD_DIM, 1),
            tables, seq)
        mix_a = _dense_attn(qa, ka, va, sg, 0, batch, seq, diff=False)
        mix_b = _window_attn(sink_b[l], qb, kb, vb, sg, BRANCH_W, batch, seq)
        mix_c = _nbr_attn(qc, kc, vc, _nbr_bias_tables(rpb_c[l], seq), sg, 2 * BRANCH_W, batch, seq)
        lam_init = 0.8 - 0.6 * math.exp(-0.3 * l)
        lam_vecs = jnp.stack([lam_q1[l], lam_k1[l], lam_q2[l], lam_k2[l]]).astype(F32)
        mix_d = _dense_attn(qd, kd, vd, sg, 3 * BRANCH_W, batch, seq, diff=True, lam_vecs=lam_vecs,
                            subln_col=subln_d[l].reshape(HEAD_DIM, 1), lam_init=lam_init)
        mixes = (mix_a, mix_b, mix_c, mix_d)
        if l == depth - 1:
            out = _out_proj(xT, mixes, w_outT, final_g.reshape(d, 1))
        else:
            xT = _out_proj(xT, mixes, w_outT)
    return out.reshape(batch, seq, d)
```

```python
import functools
import math

import jax
import jax.numpy as jnp
import numpy as np
from jax import lax
from jax.experimental import pallas as pl
from jax.experimental.pallas import tpu as pltpu

HEAD_DIM = 64
GRID_W = 64
DIFF_DIM = 32
WINDOW = 128
NA_KH = 8
NA_KW = 16
ROPE_THETA = 10000.0
EPS = 1e-6
NEG_INF = -1e30
LOG2E = math.log2(math.e)

N_HEADS = 4
BRANCH_W = N_HEADS * HEAD_DIM
KV_PAD = 128
ONES_ROWS = 16
V_EXT = HEAD_DIM + ONES_ROWS

V7X_VMEM_BYTES = 64 * 1024 * 1024
VMEM_LIMIT = V7X_VMEM_BYTES * 3 // 4

TM_PROJ = 256
TQ = 512
TK = 512
CHUNKS_PER_TRIP = 8
LANE_TILE = 256
WIN_KEYS = TQ + 2 * WINDOW
NBR_ROWS = TQ // GRID_W
NBR_KEYS = (NBR_ROWS + NA_KH) * GRID_W

F32 = jnp.float32
BF16 = jnp.bfloat16


def _params(sem):
    return pltpu.CompilerParams(dimension_semantics=sem, vmem_limit_bytes=VMEM_LIMIT)


def _transpose_kernel(x_ref, o_ref):
    o_ref[...] = x_ref[...].T


def _transpose_in(x2d):
    t, d = x2d.shape
    tm = 512
    return pl.pallas_call(
        _transpose_kernel,
        grid=(t // tm,),
        in_specs=[pl.BlockSpec((tm, d), lambda i: (i, 0))],
        out_specs=pl.BlockSpec((d, tm), lambda i: (0, i)),
        out_shape=jax.ShapeDtypeStruct((d, t), F32),
        compiler_params=_params(("parallel",)),
        name="transpose_in",
    )(x2d)


def _rot_quarters(y):
    return jnp.concatenate([y[16:32], y[0:16], y[48:64], y[32:48]], axis=0)


def _rot_halves(y):
    return jnp.concatenate([y[32:64], y[0:32]], axis=0)


def _head_rms(y, g_col):
    ms = jnp.mean(y * y, axis=0, keepdims=True)
    return y * lax.rsqrt(ms + EPS) * g_col


def _pad_rows(y, slot, n_slots):
    z = jnp.zeros_like(y)
    return jnp.concatenate([y if s == slot else z for s in range(n_slots)], axis=0)


def _silu(g):
    return g * (1.0 / (1.0 + jnp.exp(-g)))


def _in_proj_kernel(xT_ref, ng_ref, w_ref, qn_ref, kn_ref,
                    ca_ref, sa_ref, cb_ref, sb_ref, cd_ref, sd_ref,
                    qa_ref, ka_ref, va_ref, qb_ref, kb_ref, vb_ref,
                    qc_ref, kc_ref, vc_ref, qd_ref, kd_ref, vd_ref, sg_ref,
                    h_ref):
    x = xT_ref[...]
    ms = jnp.mean(x * x, axis=0, keepdims=True)
    h_ref[...] = (x * lax.rsqrt(ms + EPS) * ng_ref[...]).astype(BF16)

    def proj(r0, nrows):
        return jnp.dot(w_ref[r0:r0 + nrows, :], h_ref[...], preferred_element_type=F32)

    sc_hd = HEAD_DIM ** -0.5
    ca, sa = ca_ref[...], sa_ref[...]
    cb, sb = cb_ref[...], sb_ref[...]
    cd, sd = cd_ref[...], sd_ref[...]
    qn, kn = qn_ref[...], kn_ref[...]

    def rope_a(y):
        return y * ca + _rot_quarters(y) * sa

    def rope_b(y):
        return y * cb + _rot_halves(y) * sb

    def rope_d(y):
        return y * cd + _rot_quarters(y) * sd

    base = 0
    z = proj(base, 256)
    for hh in range(N_HEADS):
        y = rope_a(_head_rms(z[64 * hh:64 * hh + 64], qn)) * (sc_hd * LOG2E)
        qa_ref[hh] = _pad_rows(y, hh // 2, 2).astype(BF16)
    z = proj(base + 256, 128)
    kk = jnp.concatenate([rope_a(_head_rms(z[64 * j:64 * j + 64], kn)) for j in range(2)], axis=0)
    ka_ref[...] = kk.T.astype(BF16)
    va_ref[...] = proj(base + 384, 128).astype(BF16)
    sg_ref[0:256, :] = _silu(proj(base + 512, 256)).astype(BF16)

    base = 768
    z = proj(base, 256)
    for hh in range(N_HEADS):
        y = rope_b(z[64 * hh:64 * hh + 64]) * sc_hd
        qb_ref[hh] = _pad_rows(y, hh // 2, 2).astype(BF16)
    z = proj(base + 256, 128)
    kk = jnp.concatenate([rope_b(z[64 * j:64 * j + 64]) for j in range(2)], axis=0)
    kb_ref[...] = kk.T.astype(BF16)
    vb_ref[...] = proj(base + 384, 128).astype(BF16)
    sg_ref[256:512, :] = _silu(proj(base + 512, 256)).astype(BF16)

    base = 1536
    z = proj(base, 256)
    for hh in range(N_HEADS):
        y = z[64 * hh:64 * hh + 64] * sc_hd
        qc_ref[hh] = _pad_rows(y, hh % 2, 2).astype(BF16)
    z = proj(base + 256, 256)
    kc_ref[...] = z.T.astype(BF16)
    vc_ref[...] = proj(base + 512, 256).astype(BF16)
    sg_ref[512:768, :] = _silu(proj(base + 768, 256)).astype(BF16)

    base = 2560
    z = proj(base, 256)
    sc_dd = DIFF_DIM ** -0.5
    for hh in range(N_HEADS):
        y = rope_d(z[64 * hh:64 * hh + 64]) * (sc_dd * LOG2E)
        for c in range(2):
            qd_ref[2 * hh + c] = _pad_rows(y[32 * c:32 * c + 32], 2 * (hh // 2) + c, 4).astype(BF16)
    z = proj(base + 256, 128)
    kk = jnp.concatenate([rope_d(z[64 * j:64 * j + 64]) for j in range(2)], axis=0)
    kd_ref[...] = kk.T.astype(BF16)
    vd_ref[...] = proj(base + 384, 128).astype(BF16)
    sg_ref[768:1024, :] = _silu(proj(base + 512, 256)).astype(BF16)


def _in_proj(xT, ng_col, w_inT, qn_col, kn_col, tables, seq):
    d, t = xT.shape
    tm = TM_PROJ
    nt_seq = seq // tm
    d_in = w_inT.shape[0]
    full = lambda shape: pl.BlockSpec(shape, lambda i: (0,) * len(shape))
    tab = pl.BlockSpec((HEAD_DIM, tm), lambda i: (0, i % nt_seq))
    qpad = lambda n: pl.BlockSpec((n, KV_PAD, tm), lambda i: (0, 0, i))
    nat = lambda w: pl.BlockSpec((tm, w), lambda i: (i, 0))
    chan = lambda w: pl.BlockSpec((w, tm), lambda i: (0, i))
    sds = jax.ShapeDtypeStruct
    out_shape = [
        sds((N_HEADS, KV_PAD, t), BF16), sds((t, 128), BF16), sds((128, t), BF16),
        sds((N_HEADS, KV_PAD, t), BF16), sds((t, 128), BF16), sds((128, t), BF16),
        sds((N_HEADS, KV_PAD, t), BF16), sds((t, 256), BF16), sds((256, t), BF16),
        sds((2 * N_HEADS, KV_PAD, t), BF16), sds((t, 128), BF16), sds((128, t), BF16),
        sds((4 * BRANCH_W, t), BF16),
    ]
    out_specs = [
        qpad(N_HEADS), nat(128), chan(128),
        qpad(N_HEADS), nat(128), chan(128),
        qpad(N_HEADS), nat(256), chan(256),
        qpad(2 * N_HEADS), nat(128), chan(128),
        chan(4 * BRANCH_W),
    ]
    return pl.pallas_call(
        _in_proj_kernel,
        grid=(t // tm,),
        in_specs=[chan(d), full((d, 1)), full((d_in, d)), full((HEAD_DIM, 1)), full((HEAD_DIM, 1)),
                  tab, tab, tab, tab, tab, tab],
        out_specs=out_specs,
        out_shape=out_shape,
        scratch_shapes=[pltpu.VMEM((d, tm), BF16)],
        compiler_params=_params(("parallel",)),
        name="in_proj",
    )(xT, ng_col, w_inT, qn_col, kn_col, *tables)


def _v_ext(v):
    return jnp.concatenate([v, jnp.ones((ONES_ROWS, v.shape[1]), v.dtype)], axis=0)


def _dense_kernel(*refs, n_q, diff, lam_init, seq):
    if diff:
        (q_ref, k_ref, v_ref, sg_ref, lam_ref, sub_ref, o_ref, s0_ref, s1_ref, acc_ref) = refs
    else:
        (q_ref, k_ref, v_ref, sg_ref, o_ref, s0_ref, s1_ref, acc_ref) = refs
    s_slots = (s0_ref, s1_ref)
    n_chunks = seq // TK

    acc_ref[...] = jnp.zeros(acc_ref.shape, F32)

    tiles = [(i, slice(LANE_TILE * h, LANE_TILE * (h + 1))) for i in range(n_q) for h in range(TQ // LANE_TILE)]

    def score_tile(slot, chunk, i, lanes):
        k = k_ref[pl.ds(pl.multiple_of(chunk * TK, TK), TK), :]
        s = jnp.dot(k, q_ref[i, :, lanes], preferred_element_type=F32).astype(BF16)
        s_slots[slot][i, :, lanes] = s
        return jnp.max(s, axis=0, keepdims=True).astype(F32)

    def value_tile(slot, chunk, i, lanes, m_old, m_new):
        v = _v_ext(v_ref[:, pl.ds(pl.multiple_of(chunk * TK, TK), TK)])
        alpha = jnp.exp2(m_old - m_new)
        p = jnp.exp2(s_slots[slot][i, :, lanes] - m_new.astype(BF16))
        acc_ref[i, :, lanes] = alpha * acc_ref[i, :, lanes] + jnp.dot(v, p, preferred_element_type=F32)

    def body(jj, carry):
        m_old, m_run = carry
        c0 = CHUNKS_PER_TRIP * jj
        for u in range(CHUNKS_PER_TRIP):
            c_next = jnp.minimum(c0 + u + 1, n_chunks - 1)
            nxt = []
            for t, (i, lanes) in enumerate(tiles):
                nxt.append(score_tile((u + 1) % 2, c_next, i, lanes))
                value_tile(u % 2, c0 + u, i, lanes, m_old[t], m_run[t])
            m_old, m_run = m_run, [jnp.maximum(a, b) for a, b in zip(m_run, nxt)]
        return m_old, m_run

    first = [score_tile(0, 0, i, lanes) for i, lanes in tiles]
    minus_inf = [jnp.full((1, LANE_TILE), -jnp.inf, F32)] * len(tiles)
    lax.fori_loop(0, n_chunks // CHUNKS_PER_TRIP, body, (minus_inf, first))

    def normalized(i):
        acc = acc_ref[i]
        return acc[0:HEAD_DIM] / acc[HEAD_DIM:HEAD_DIM + 1]

    if diff:
        lv = lam_ref[...]
        lam = (jnp.exp(jnp.sum(lv[0:1] * lv[1:2], axis=1, keepdims=True))
               - jnp.exp(jnp.sum(lv[2:3] * lv[3:4], axis=1, keepdims=True)) + lam_init)
        o = normalized(0) - lam * normalized(1)
        ms = jnp.mean(o * o, axis=0, keepdims=True)
        o = o * lax.rsqrt(ms + EPS) * sub_ref[...] * (1.0 - lam_init)
        o_ref[...] = (o * sg_ref[...].astype(F32)).astype(BF16)
    else:
        for i in range(n_q):
            r = slice(HEAD_DIM * i, HEAD_DIM * (i + 1))
            o_ref[r, :] = (normalized(i) * sg_ref[r, :].astype(F32)).astype(BF16)


def _dense_attn(q_pad, k_nat, vT, sgT, sg_row0, batch, seq, diff, lam_vecs=None, subln_col=None,
                lam_init=0.0):
    nt = seq // TQ
    n_q = 2
    out_rows = HEAD_DIM if diff else 2 * HEAD_DIM
    n_groups = BRANCH_W // out_rows
    sg_blk0 = sg_row0 // out_rows
    in_specs = [
        pl.BlockSpec((n_q, KV_PAD, TQ), lambda b, g, i: (g, 0, b * nt + i)),
        pl.BlockSpec((seq, KV_PAD), lambda b, g, i: (b, 0)),
        pl.BlockSpec((HEAD_DIM, seq), (lambda b, g, i: (g // 2, b)) if diff else (lambda b, g, i: (g, b))),
        pl.BlockSpec((out_rows, TQ), lambda b, g, i: (sg_blk0 + g, b * nt + i)),
    ]
    args = [q_pad, k_nat, vT, sgT]
    if diff:
        in_specs += [pl.BlockSpec((4, DIFF_DIM), lambda b, g, i: (0, 0)),
                     pl.BlockSpec((HEAD_DIM, 1), lambda b, g, i: (0, 0))]
        args += [lam_vecs, subln_col]
    return pl.pallas_call(
        functools.partial(_dense_kernel, n_q=n_q, diff=diff, lam_init=lam_init, seq=seq),
        grid=(batch, n_groups, nt),
        in_specs=in_specs,
        out_specs=pl.BlockSpec((out_rows, TQ), lambda b, g, i: (g, b * nt + i)),
        out_shape=jax.ShapeDtypeStruct((BRANCH_W, batch * seq), BF16),
        scratch_shapes=[pltpu.VMEM((n_q, TK, TQ), BF16), pltpu.VMEM((n_q, TK, TQ), BF16),
                        pltpu.VMEM((n_q, V_EXT, TQ), F32)],
        compiler_params=_params(("parallel", "parallel", "parallel")),
        name="diff_attn" if diff else "dense_attn",
    )(*args)


def _window_kernel(sink_ref, q_ref, k_ref, v_ref, sg_ref, o_ref, *, seq):
    h = pl.program_id(1)
    q0 = pl.program_id(2) * TQ
    start = pl.multiple_of(jnp.clip(q0 - WINDOW, 0, seq - WIN_KEYS), WINDOW)
    k = k_ref[pl.ds(start, WIN_KEYS), :]
    v = _v_ext(v_ref[:, pl.ds(start, WIN_KEYS)])
    s = jnp.dot(k, q_ref[0], preferred_element_type=F32)
    kpos = start + lax.broadcasted_iota(jnp.int32, s.shape, 0)
    qpos = q0 + lax.broadcasted_iota(jnp.int32, s.shape, 1)
    s = jnp.where(jnp.abs(qpos - kpos) <= WINDOW, s, NEG_INF)
    sink = sink_ref[h]
    m = jnp.maximum(jnp.max(s, axis=0, keepdims=True), sink)
    p = jnp.exp(s - m).astype(BF16)
    acc = jnp.dot(v, p, preferred_element_type=F32)
    denom = acc[HEAD_DIM:HEAD_DIM + 1] + jnp.exp(sink - m)
    o_ref[...] = (acc[0:HEAD_DIM] / denom * sg_ref[...].astype(F32)).astype(BF16)


def _window_attn(sink, q_pad, k_nat, vT, sgT, sg_row0, batch, seq):
    nt = seq // TQ
    sg_blk0 = sg_row0 // HEAD_DIM
    return pl.pallas_call(
        functools.partial(_window_kernel, seq=seq),
        grid=(batch, N_HEADS, nt),
        in_specs=[
            pl.BlockSpec(memory_space=pltpu.SMEM),
            pl.BlockSpec((1, KV_PAD, TQ), lambda b, h, i: (h, 0, b * nt + i)),
            pl.BlockSpec((seq, KV_PAD), lambda b, h, i: (b, 0)),
            pl.BlockSpec((HEAD_DIM, seq), lambda b, h, i: (h // 2, b)),
            pl.BlockSpec((HEAD_DIM, TQ), lambda b, h, i: (sg_blk0 + h, b * nt + i)),
        ],
        out_specs=pl.BlockSpec((HEAD_DIM, TQ), lambda b, h, i: (h, b * nt + i)),
        out_shape=jax.ShapeDtypeStruct((BRANCH_W, batch * seq), BF16),
        compiler_params=_params(("parallel", "parallel", "parallel")),
        name="window_attn",
    )(sink, q_pad, k_nat, vT, sgT)


def _nbr_start(tile, seq):
    return jnp.clip(tile * TQ - (NA_KH // 2) * GRID_W, 0, seq - NBR_KEYS)


def _nbr_kernel(q_ref, k_ref, v_ref, bias_ref, sg_ref, o_ref, *, seq):
    start = pl.multiple_of(_nbr_start(pl.program_id(2), seq), GRID_W * (NA_KH // 2))
    k = k_ref[pl.ds(start, NBR_KEYS), :]
    v = _v_ext(v_ref[:, pl.ds(start, NBR_KEYS)])
    s = jnp.dot(k, q_ref[0], preferred_element_type=F32) + bias_ref[0, 0]
    m = jnp.max(s, axis=0, keepdims=True)
    p = jnp.exp(s - m).astype(BF16)
    acc = jnp.dot(v, p, preferred_element_type=F32)
    o_ref[...] = (acc[0:HEAD_DIM] / acc[HEAD_DIM:HEAD_DIM + 1] * sg_ref[...].astype(F32)).astype(BF16)


N_DR = 2 * NA_KH - 1
N_DC = 2 * NA_KW - 1


def _nbr_bias_kernel(rpb_ref, o_ref, *, seq):
    h = pl.program_id(0)
    nt = seq // TQ
    rows = seq // GRID_W
    kh = min(NA_KH, rows)
    shape = (GRID_W, 2 * GRID_W)
    kc = lax.broadcasted_iota(jnp.int32, shape, 0)
    lane = lax.broadcasted_iota(jnp.int32, shape, 1)
    qc = lane % GRID_W
    dcol = jnp.clip(kc - qc, -(NA_KW - 1), NA_KW - 1) + NA_KW - 1
    cs = jnp.clip(qc - NA_KW // 2, 0, GRID_W - NA_KW)
    col_ok = (kc >= cs) & (kc < cs + NA_KW)
    hits = [dcol == j for j in range(N_DC)]
    toeplitz = []
    for dr in range(N_DR):
        t = jnp.zeros(shape, F32)
        for j in range(N_DC):
            t = jnp.where(hits[j], rpb_ref[(h * N_DR + dr) * N_DC + j], t)
        toeplitz.append(jnp.where(col_ok, t, NEG_INF))
    masked = jnp.full(shape, NEG_INF, F32)
    left = lane < GRID_W
    for vi, tile in enumerate((0, 1, nt - 1)):
        start_row = int(np.clip(tile * NBR_ROWS - NA_KH // 2, 0, rows - NBR_KEYS // GRID_W))
        for kr in range(NBR_KEYS // GRID_W):
            krow = start_row + kr
            for qp in range(NBR_ROWS // 2):
                halves = []
                for qrow in (tile * NBR_ROWS + 2 * qp, tile * NBR_ROWS + 2 * qp + 1):
                    rs = int(np.clip(qrow - kh // 2, 0, rows - kh))
                    halves.append(toeplitz[krow - qrow + NA_KH - 1] if rs <= krow < rs + kh else masked)
                blk = halves[0] if halves[0] is halves[1] else jnp.where(left, halves[0], halves[1])
                o_ref[vi, 0, GRID_W * kr:GRID_W * (kr + 1), 2 * GRID_W * qp:2 * GRID_W * (qp + 1)] = blk


def _nbr_bias_tables(rpb, seq):
    return pl.pallas_call(
        functools.partial(_nbr_bias_kernel, seq=seq),
        grid=(N_HEADS,),
        in_specs=[pl.BlockSpec(memory_space=pltpu.SMEM)],
        out_specs=pl.BlockSpec((3, 1, NBR_KEYS, TQ), lambda h: (0, h, 0, 0)),
        out_shape=jax.ShapeDtypeStruct((3, N_HEADS, NBR_KEYS, TQ), F32),
        compiler_params=_params(("parallel",)),
        name="nbr_bias",
    )(rpb.astype(F32).reshape(-1))


def _nbr_attn(q_pad, k_nat, vT, bias, sgT, sg_row0, batch, seq):
    nt = seq // TQ
    sg_blk0 = sg_row0 // HEAD_DIM

    def variant(i):
        return jnp.where(i == 0, 0, jnp.where(i == nt - 1, 2, 1))

    return pl.pallas_call(
        functools.partial(_nbr_kernel, seq=seq),
        grid=(batch, N_HEADS, nt),
        in_specs=[
            pl.BlockSpec((1, KV_PAD, TQ), lambda b, h, i: (h, 0, b * nt + i)),
            pl.BlockSpec((seq, KV_PAD), lambda b, h, i: (b, h // 2)),
            pl.BlockSpec((HEAD_DIM, seq), lambda b, h, i: (h, b)),
            pl.BlockSpec((1, 1, NBR_KEYS, TQ), lambda b, h, i: (variant(i), h, 0, 0)),
            pl.BlockSpec((HEAD_DIM, TQ), lambda b, h, i: (sg_blk0 + h, b * nt + i)),
        ],
        out_specs=pl.BlockSpec((HEAD_DIM, TQ), lambda b, h, i: (h, b * nt + i)),
        out_shape=jax.ShapeDtypeStruct((BRANCH_W, batch * seq), BF16),
        compiler_params=_params(("parallel", "parallel", "parallel")),
        name="nbr_attn",
    )(q_pad, k_nat, vT, bias, sgT)


def _out_proj_kernel(xT_ref, ma_ref, mb_ref, mc_ref, md_ref, w_ref, *rest, final):
    acc = xT_ref[...]
    for j, m_ref in enumerate((ma_ref, mb_ref, mc_ref, md_ref)):
        acc = acc + jnp.dot(w_ref[:, BRANCH_W * j:BRANCH_W * (j + 1)], m_ref[...],
                            preferred_element_type=F32)
    if final:
        fg_ref, o_ref = rest
        ms = jnp.mean(acc * acc, axis=0, keepdims=True)
        o_ref[...] = (acc * lax.rsqrt(ms + EPS) * fg_ref[...]).T
    else:
        (o_ref,) = rest
        o_ref[...] = acc


def _out_proj(xT, mixes, w_outT, final_g_col=None):
    d, t = xT.shape
    tm = TM_PROJ
    final = final_g_col is not None
    chan = lambda w: pl.BlockSpec((w, tm), lambda i: (0, i))
    in_specs = [chan(d)] + [chan(BRANCH_W)] * 4 + [pl.BlockSpec((d, 4 * BRANCH_W), lambda i: (0, 0))]
    args = [xT, *mixes, w_outT]
    if final:
        in_specs.append(pl.BlockSpec((d, 1), lambda i: (0, 0)))
        args.append(final_g_col)
        out_spec = pl.BlockSpec((tm, d), lambda i: (i, 0))
        out_shape = jax.ShapeDtypeStruct((t, d), F32)
    else:
        out_spec = chan(d)
        out_shape = jax.ShapeDtypeStruct((d, t), F32)
    return pl.pallas_call(
        functools.partial(_out_proj_kernel, final=final),
        grid=(t // tm,),
        in_specs=in_specs,
        out_specs=out_spec,
        out_shape=out_shape,
        compiler_params=_params(("parallel",)),
        name="out_proj_final" if final else "out_proj",
    )(*args)


def _rope_tables(seq):
    t = jnp.arange(seq)
    row, col = t // GRID_W, t % GRID_W

    def cs(pos, d):
        inv = jnp.power(ROPE_THETA, -jnp.arange(0, d, 2, dtype=F32) / d)
        ang = pos.astype(F32)[None, :] * inv[:, None]
        return jnp.cos(ang), jnp.sin(ang)

    cr, sr = cs(row, HEAD_DIM // 2)
    cc, sc = cs(col, HEAD_DIM // 2)
    ct, st = cs(t, HEAD_DIM)
    cdt, sdt = cs(t, DIFF_DIM)
    cat = lambda *a: jnp.concatenate(a, axis=0)
    return (cat(cr, cr, cc, cc), cat(-sr, sr, -sc, sc),
            cat(ct, ct), cat(-st, st),
            cat(cdt, cdt, cdt, cdt), cat(-sdt, sdt, -sdt, sdt))


def kernel(x, norm_g, w_in, w_out, qn_a, kn_a, sink_b, rpb_c, lam_q1, lam_k1, lam_q2, lam_k2, subln_d, final_g):
    batch, seq, d = x.shape
    depth = w_in.shape[0]
    tables = _rope_tables(seq)
    xT = _transpose_in(x.reshape(batch * seq, d))
    out = None
    for l in range(depth):
        w_inT = w_in[l].T.astype(BF16)
        w_outT = w_out[l].T.astype(BF16)
        (qa, ka, va, qb, kb, vb, qc, kc, vc, qd, kd, vd, sg) = _in_proj(
            xT, norm_g[l].reshape(d, 1), w_inT, qn_a[l].reshape(HEAD_DIM, 1), kn_a[l].reshape(HEAD_DIM, 1),
            tables, seq)
        mix_a = _dense_attn(qa, ka, va, sg, 0, batch, seq, diff=False)
        mix_b = _window_attn(sink_b[l], qb, kb, vb, sg, BRANCH_W, batch, seq)
        mix_c = _nbr_attn(qc, kc, vc, _nbr_bias_tables(rpb_c[l], seq), sg, 2 * BRANCH_W, batch, seq)
        lam_init = 0.8 - 0.6 * math.exp(-0.3 * l)
        lam_vecs = jnp.stack([lam_q1[l], lam_k1[l], lam_q2[l], lam_k2[l]]).astype(F32)
        mix_d = _dense_attn(qd, kd, vd, sg, 3 * BRANCH_W, batch, seq, diff=True, lam_vecs=lam_vecs,
                            subln_col=subln_d[l].reshape(HEAD_DIM, 1), lam_init=lam_init)
        mixes = (mix_a, mix_b, mix_c, mix_d)
        if l == depth - 1:
            out = _out_proj(xT, mixes, w_outT, final_g.reshape(d, 1))
        else:
            xT = _out_proj(xT, mixes, w_outT)
    return out.reshape(batch, seq, d)
```

```python
import functools
import math

import jax
import jax.numpy as jnp
import numpy as np
from jax import lax
from jax.experimental import pallas as pl
from jax.experimental.pallas import tpu as pltpu

HEAD_DIM = 64
GRID_W = 64
DIFF_DIM = 32
WINDOW = 128
NA_KH = 8
NA_KW = 16
ROPE_THETA = 10000.0
EPS = 1e-6
NEG_INF = -1e30
LOG2E = math.log2(math.e)

N_HEADS = 4
BRANCH_W = N_HEADS * HEAD_DIM
KV_PAD = 128
ONES_ROWS = 16
V_EXT = HEAD_DIM + ONES_ROWS

V7X_VMEM_BYTES = 64 * 1024 * 1024
VMEM_LIMIT = V7X_VMEM_BYTES * 3 // 4

TM_PROJ = 256
TM_OUT_PROJ = 512
TQ = 512
TK = 512
CHUNKS_PER_TRIP = 16
LANE_TILE = 256
WIN_KEYS = LANE_TILE + 2 * WINDOW
NBR_TILE_ROWS = LANE_TILE // GRID_W
NBR_KEY_ROWS = NBR_TILE_ROWS + NA_KH
NBR_KEYS = NBR_KEY_ROWS * GRID_W

F32 = jnp.float32
BF16 = jnp.bfloat16


def _params(sem):
    return pltpu.CompilerParams(dimension_semantics=sem, vmem_limit_bytes=VMEM_LIMIT)


def _rot_quarters(y):
    return jnp.concatenate([y[16:32], y[0:16], y[48:64], y[32:48]], axis=0)


def _rot_halves(y):
    return jnp.concatenate([y[32:64], y[0:32]], axis=0)


def _head_rms(y, g_col):
    ms = jnp.mean(y * y, axis=0, keepdims=True)
    return y * lax.rsqrt(ms + EPS) * g_col


def _pad_rows(y, slot, n_slots):
    z = jnp.zeros_like(y)
    return jnp.concatenate([y if s == slot else z for s in range(n_slots)], axis=0)


def _silu(g):
    return g * (1.0 / (1.0 + jnp.exp(-g)))


def _in_proj_kernel(x_ref, ng_ref, w_ref, qn_ref, kn_ref,
                    ca_ref, sa_ref, cb_ref, sb_ref, cd_ref, sd_ref,
                    qa_ref, ka_ref, va_ref, qb_ref, kb_ref, vb_ref,
                    qc_ref, kc_ref, vc_ref, qd_ref, kd_ref, vd_ref, sg_ref,
                    h_ref, *, token_major):
    x = x_ref[...].T if token_major else x_ref[...]
    ms = jnp.mean(x * x, axis=0, keepdims=True)
    h_ref[...] = (x * lax.rsqrt(ms + EPS) * ng_ref[...]).astype(BF16)

    def proj(r0, nrows):
        return jnp.dot(w_ref[r0:r0 + nrows, :], h_ref[...], preferred_element_type=F32)

    sc_hd = HEAD_DIM ** -0.5
    ca, sa = ca_ref[...], sa_ref[...]
    cb, sb = cb_ref[...], sb_ref[...]
    cd, sd = cd_ref[...], sd_ref[...]
    qn, kn = qn_ref[...], kn_ref[...]

    def rope_a(y):
        return y * ca + _rot_quarters(y) * sa

    def rope_b(y):
        return y * cb + _rot_halves(y) * sb

    def rope_d(y):
        return y * cd + _rot_quarters(y) * sd

    base = 0
    z = proj(base, 256)
    for hh in range(N_HEADS):
        y = rope_a(_head_rms(z[64 * hh:64 * hh + 64], qn)) * (sc_hd * LOG2E)
        qa_ref[hh] = _pad_rows(y, hh // 2, 2).astype(BF16)
    z = proj(base + 256, 128)
    kk = jnp.concatenate([rope_a(_head_rms(z[64 * j:64 * j + 64], kn)) for j in range(2)], axis=0)
    ka_ref[...] = kk.T.astype(BF16)
    va_ref[...] = proj(base + 384, 128).astype(BF16)
    sg_ref[0:256, :] = _silu(proj(base + 512, 256)).astype(BF16)

    base = 768
    z = proj(base, 256)
    for hh in range(N_HEADS):
        y = rope_b(z[64 * hh:64 * hh + 64]) * (sc_hd * LOG2E)
        qb_ref[hh] = _pad_rows(y, hh // 2, 2).astype(BF16)
    z = proj(base + 256, 128)
    kk = jnp.concatenate([rope_b(z[64 * j:64 * j + 64]) for j in range(2)], axis=0)
    kb_ref[...] = kk.T.astype(BF16)
    vb_ref[...] = proj(base + 384, 128).astype(BF16)
    sg_ref[256:512, :] = _silu(proj(base + 512, 256)).astype(BF16)

    base = 1536
    z = proj(base, 256)
    for hh in range(N_HEADS):
        y = z[64 * hh:64 * hh + 64] * (sc_hd * LOG2E)
        qc_ref[hh] = _pad_rows(y, hh % 2, 2).astype(BF16)
    z = proj(base + 256, 256)
    kc_ref[...] = z.T.astype(BF16)
    vc_ref[...] = proj(base + 512, 256).astype(BF16)
    sg_ref[512:768, :] = _silu(proj(base + 768, 256)).astype(BF16)

    base = 2560
    z = proj(base, 256)
    sc_dd = DIFF_DIM ** -0.5
    for hh in range(N_HEADS):
        y = rope_d(z[64 * hh:64 * hh + 64]) * (sc_dd * LOG2E)
        for c in range(2):
            qd_ref[2 * hh + c] = _pad_rows(y[32 * c:32 * c + 32], 2 * (hh // 2) + c, 4).astype(BF16)
    z = proj(base + 256, 128)
    kk = jnp.concatenate([rope_d(z[64 * j:64 * j + 64]) for j in range(2)], axis=0)
    kd_ref[...] = kk.T.astype(BF16)
    vd_ref[...] = proj(base + 384, 128).astype(BF16)
    sg_ref[768:1024, :] = _silu(proj(base + 512, 256)).astype(BF16)


def _in_proj(x, ng_col, w_inT, qn_col, kn_col, tables, seq, token_major):
    t, d = x.shape if token_major else x.shape[::-1]
    tm = TM_PROJ
    nt_seq = seq // tm
    d_in = w_inT.shape[0]
    full = lambda shape: pl.BlockSpec(shape, lambda i: (0,) * len(shape))
    tab = pl.BlockSpec((HEAD_DIM, tm), lambda i: (0, i % nt_seq))
    qpad = lambda n: pl.BlockSpec((n, KV_PAD, tm), lambda i: (0, 0, i))
    nat = lambda w: pl.BlockSpec((tm, w), lambda i: (i, 0))
    chan = lambda w: pl.BlockSpec((w, tm), lambda i: (0, i))
    sds = jax.ShapeDtypeStruct
    out_shape = [
        sds((N_HEADS, KV_PAD, t), BF16), sds((t, 128), BF16), sds((128, t), BF16),
        sds((N_HEADS, KV_PAD, t), BF16), sds((t, 128), BF16), sds((128, t), BF16),
        sds((N_HEADS, KV_PAD, t), BF16), sds((t, 256), BF16), sds((256, t), BF16),
        sds((2 * N_HEADS, KV_PAD, t), BF16), sds((t, 128), BF16), sds((128, t), BF16),
        sds((4 * BRANCH_W, t), BF16),
    ]
    out_specs = [
        qpad(N_HEADS), nat(128), chan(128),
        qpad(N_HEADS), nat(128), chan(128),
        qpad(N_HEADS), nat(256), chan(256),
        qpad(2 * N_HEADS), nat(128), chan(128),
        chan(4 * BRANCH_W),
    ]
    return pl.pallas_call(
        functools.partial(_in_proj_kernel, token_major=token_major),
        grid=(t // tm,),
        in_specs=[nat(d) if token_major else chan(d), full((d, 1)), full((d_in, d)),
                  full((HEAD_DIM, 1)), full((HEAD_DIM, 1)), tab, tab, tab, tab, tab, tab],
        out_specs=out_specs,
        out_shape=out_shape,
        scratch_shapes=[pltpu.VMEM((d, tm), BF16)],
        compiler_params=_params(("parallel",)),
        name="in_proj",
    )(x, ng_col, w_inT, qn_col, kn_col, *tables)


def _v_ext(v):
    return jnp.concatenate([v, jnp.ones((ONES_ROWS, v.shape[1]), v.dtype)], axis=0)


def _dense_kernel(*refs, n_q, diff, lam_init, seq):
    if diff:
        (q_ref, k_ref, v_ref, sg_ref, lam_ref, sub_ref, o_ref, s0_ref, s1_ref, acc_ref) = refs
    else:
        (q_ref, k_ref, v_ref, sg_ref, o_ref, s0_ref, s1_ref, acc_ref) = refs
    s_slots = (s0_ref, s1_ref)
    n_chunks = seq // TK

    acc_ref[...] = jnp.zeros(acc_ref.shape, F32)

    tiles = [(i, slice(LANE_TILE * h, LANE_TILE * (h + 1))) for i in range(n_q) for h in range(TQ // LANE_TILE)]

    def score_tile(slot, chunk, i, lanes):
        k = k_ref[pl.ds(pl.multiple_of(chunk * TK, TK), TK), :]
        s = jnp.dot(k, q_ref[i, :, lanes], preferred_element_type=F32).astype(BF16)
        s_slots[slot][i, :, lanes] = s
        return jnp.max(s, axis=0, keepdims=True).astype(F32)

    def value_tile(slot, chunk, i, lanes, m_old, m_new):
        v = _v_ext(v_ref[:, pl.ds(pl.multiple_of(chunk * TK, TK), TK)])
        alpha = jnp.exp2(m_old - m_new)
        p = jnp.exp2(s_slots[slot][i, :, lanes] - m_new.astype(BF16))
        acc_ref[i, :, lanes] = alpha * acc_ref[i, :, lanes] + jnp.dot(v, p, preferred_element_type=F32)

    def body(jj, carry):
        m_old, m_run = carry
        c0 = CHUNKS_PER_TRIP * jj
        for u in range(CHUNKS_PER_TRIP):
            c_next = jnp.minimum(c0 + u + 1, n_chunks - 1)
            nxt = []
            for t, (i, lanes) in enumerate(tiles):
                nxt.append(score_tile((u + 1) % 2, c_next, i, lanes))
                value_tile(u % 2, c0 + u, i, lanes, m_old[t], m_run[t])
            m_old, m_run = m_run, [jnp.maximum(a, b) for a, b in zip(m_run, nxt)]
        return m_old, m_run

    first = [score_tile(0, 0, i, lanes) for i, lanes in tiles]
    minus_inf = [jnp.full((1, LANE_TILE), -jnp.inf, F32)] * len(tiles)
    lax.fori_loop(0, n_chunks // CHUNKS_PER_TRIP, body, (minus_inf, first))

    def normalized(i):
        acc = acc_ref[i]
        return acc[0:HEAD_DIM] / acc[HEAD_DIM:HEAD_DIM + 1]

    if diff:
        lv = lam_ref[...]
        lam = (jnp.exp(jnp.sum(lv[0:1] * lv[1:2], axis=1, keepdims=True))
               - jnp.exp(jnp.sum(lv[2:3] * lv[3:4], axis=1, keepdims=True)) + lam_init)
        o = normalized(0) - lam * normalized(1)
        ms = jnp.mean(o * o, axis=0, keepdims=True)
        o = o * lax.rsqrt(ms + EPS) * sub_ref[...] * (1.0 - lam_init)
        o_ref[...] = (o * sg_ref[...].astype(F32)).astype(BF16)
    else:
        for i in range(n_q):
            r = slice(HEAD_DIM * i, HEAD_DIM * (i + 1))
            o_ref[r, :] = (normalized(i) * sg_ref[r, :].astype(F32)).astype(BF16)


def _dense_attn(q_pad, k_nat, vT, sgT, sg_row0, batch, seq, diff, lam_vecs=None, subln_col=None,
                lam_init=0.0):
    nt = seq // TQ
    n_q = 2
    out_rows = HEAD_DIM if diff else 2 * HEAD_DIM
    n_groups = BRANCH_W // out_rows
    sg_blk0 = sg_row0 // out_rows
    in_specs = [
        pl.BlockSpec((n_q, KV_PAD, TQ), lambda b, g, i: (g, 0, b * nt + i)),
        pl.BlockSpec((seq, KV_PAD), lambda b, g, i: (b, 0)),
        pl.BlockSpec((HEAD_DIM, seq), (lambda b, g, i: (g // 2, b)) if diff else (lambda b, g, i: (g, b))),
        pl.BlockSpec((out_rows, TQ), lambda b, g, i: (sg_blk0 + g, b * nt + i)),
    ]
    args = [q_pad, k_nat, vT, sgT]
    if diff:
        in_specs += [pl.BlockSpec((4, DIFF_DIM), lambda b, g, i: (0, 0)),
                     pl.BlockSpec((HEAD_DIM, 1), lambda b, g, i: (0, 0))]
        args += [lam_vecs, subln_col]
    return pl.pallas_call(
        functools.partial(_dense_kernel, n_q=n_q, diff=diff, lam_init=lam_init, seq=seq),
        grid=(batch, n_groups, nt),
        in_specs=in_specs,
        out_specs=pl.BlockSpec((out_rows, TQ), lambda b, g, i: (g, b * nt + i)),
        out_shape=jax.ShapeDtypeStruct((BRANCH_W, batch * seq), BF16),
        scratch_shapes=[pltpu.VMEM((n_q, TK, TQ), BF16), pltpu.VMEM((n_q, TK, TQ), BF16),
                        pltpu.VMEM((n_q, V_EXT, TQ), F32)],
        compiler_params=_params(("parallel", "parallel", "parallel")),
        name="diff_attn" if diff else "dense_attn",
    )(*args)


def _softmax_value_tile(sb, v, extra_logit=None):
    m = jnp.max(sb, axis=0, keepdims=True)
    if extra_logit is not None:
        m = jnp.maximum(m.astype(F32), extra_logit).astype(BF16)
    p = jnp.exp2(sb - m)
    acc = jnp.dot(v, p, preferred_element_type=F32)
    denom = acc[HEAD_DIM:HEAD_DIM + 1]
    if extra_logit is not None:
        denom = denom + jnp.exp2(extra_logit - m.astype(F32))
    return acc[0:HEAD_DIM], denom


def _window_kernel(sink_ref, q_ref, k_ref, v_ref, sg_ref, o_ref, *, seq):
    g = pl.program_id(1)
    q0 = pl.program_id(2) * TQ
    shape = (WIN_KEYS, LANE_TILE)
    rel = lax.broadcasted_iota(jnp.int32, shape, 0) - lax.broadcasted_iota(jnp.int32, shape, 1)
    pending = []
    for h in range(TQ // LANE_TILE):
        lanes = slice(LANE_TILE * h, LANE_TILE * (h + 1))
        qs = q0 + LANE_TILE * h
        start = pl.multiple_of(jnp.clip(qs - WINDOW, 0, seq - WIN_KEYS), WINDOW)
        k = k_ref[pl.ds(start, WIN_KEYS), :]
        inside = jnp.abs(rel + (start - qs)) <= WINDOW
        for i in range(2):
            s = jnp.dot(k, q_ref[i, :, lanes], preferred_element_type=F32)
            pending.append((i, lanes, start, jnp.where(inside, s, NEG_INF).astype(BF16)))
    for i, lanes, start, sb in pending:
        rows = slice(HEAD_DIM * i, HEAD_DIM * (i + 1))
        v = _v_ext(v_ref[:, pl.ds(start, WIN_KEYS)])
        out, denom = _softmax_value_tile(sb, v, sink_ref[2 * g + i] * LOG2E)
        o_ref[rows, lanes] = (out / denom * sg_ref[rows, lanes].astype(F32)).astype(BF16)


def _window_attn(sink, q_pad, k_nat, vT, sgT, sg_row0, batch, seq):
    nt = seq // TQ
    sg_blk0 = sg_row0 // (2 * HEAD_DIM)
    return pl.pallas_call(
        functools.partial(_window_kernel, seq=seq),
        grid=(batch, N_HEADS // 2, nt),
        in_specs=[
            pl.BlockSpec(memory_space=pltpu.SMEM),
            pl.BlockSpec((2, KV_PAD, TQ), lambda b, g, i: (g, 0, b * nt + i)),
            pl.BlockSpec((seq, KV_PAD), lambda b, g, i: (b, 0)),
            pl.BlockSpec((HEAD_DIM, seq), lambda b, g, i: (g, b)),
            pl.BlockSpec((2 * HEAD_DIM, TQ), lambda b, g, i: (sg_blk0 + g, b * nt + i)),
        ],
        out_specs=pl.BlockSpec((2 * HEAD_DIM, TQ), lambda b, g, i: (g, b * nt + i)),
        out_shape=jax.ShapeDtypeStruct((BRANCH_W, batch * seq), BF16),
        compiler_params=_params(("parallel", "parallel", "parallel")),
        name="window_attn",
    )(sink, q_pad, k_nat, vT, sgT)


N_DR = 2 * NA_KH - 1
N_DC = 2 * NA_KW - 1


def _nbr_start_row(first_row, rows, clip):
    return clip(first_row - NA_KH // 2, 0, rows - NBR_KEY_ROWS)


def _nbr_kernel(q_ref, k_ref, v_ref, bias_lo_ref, bias_hi_ref, sg_ref, o_ref, *, seq):
    rows = seq // GRID_W
    r0 = pl.program_id(2) * (TQ // GRID_W)
    pending = []
    for h, bias_ref in enumerate((bias_lo_ref, bias_hi_ref)):
        lanes = slice(LANE_TILE * h, LANE_TILE * (h + 1))
        start = _nbr_start_row(r0 + NBR_TILE_ROWS * h, rows, jnp.clip) * GRID_W
        start = pl.multiple_of(start, NBR_TILE_ROWS * GRID_W)
        k = k_ref[pl.ds(start, NBR_KEYS), :]
        for i in range(2):
            s = jnp.dot(k, q_ref[i, :, lanes], preferred_element_type=F32) + bias_ref[0, i]
            pending.append((i, lanes, start, s.astype(BF16)))
    for i, lanes, start, sb in pending:
        hrows = slice(HEAD_DIM * i, HEAD_DIM * (i + 1))
        v = _v_ext(v_ref[hrows, pl.ds(start, NBR_KEYS)])
        out, denom = _softmax_value_tile(sb, v)
        o_ref[hrows, lanes] = (out / denom * sg_ref[hrows, lanes].astype(F32)).astype(BF16)


def _nbr_bias_kernel(rpb_ref, o_ref, *, seq):
    h = pl.program_id(0)
    rows = seq // GRID_W
    kh = min(NA_KH, rows)
    shape = (GRID_W, 2 * GRID_W)
    kc = lax.broadcasted_iota(jnp.int32, shape, 0)
    lane = lax.broadcasted_iota(jnp.int32, shape, 1)
    qc = lane % GRID_W
    dcol = jnp.clip(kc - qc, -(NA_KW - 1), NA_KW - 1) + NA_KW - 1
    cs = jnp.clip(qc - NA_KW // 2, 0, GRID_W - NA_KW)
    col_ok = (kc >= cs) & (kc < cs + NA_KW)
    hits = [dcol == j for j in range(N_DC)]
    toeplitz = []
    for dr in range(N_DR):
        t = jnp.zeros(shape, F32)
        for j in range(N_DC):
            t = jnp.where(hits[j], rpb_ref[(h * N_DR + dr) * N_DC + j], t)
        toeplitz.append(jnp.where(col_ok, t * LOG2E, NEG_INF))
    masked = jnp.full(shape, NEG_INF, F32)
    left = lane < GRID_W
    for vi, first_row in enumerate((0, NBR_TILE_ROWS, rows - NBR_TILE_ROWS)):
        start_row = int(_nbr_start_row(first_row, rows, np.clip))
        for kr in range(NBR_KEY_ROWS):
            krow = start_row + kr
            for qp in range(NBR_TILE_ROWS // 2):
                halves = []
                for qrow in (first_row + 2 * qp, first_row + 2 * qp + 1):
                    rs = int(np.clip(qrow - kh // 2, 0, rows - kh))
                    halves.append(toeplitz[krow - qrow + NA_KH - 1] if rs <= krow < rs + kh else masked)
                blk = halves[0] if halves[0] is halves[1] else jnp.where(left, halves[0], halves[1])
                o_ref[vi, 0, GRID_W * kr:GRID_W * (kr + 1), 2 * GRID_W * qp:2 * GRID_W * (qp + 1)] = blk


def _nbr_bias_tables(rpb, seq):
    return pl.pallas_call(
        functools.partial(_nbr_bias_kernel, seq=seq),
        grid=(N_HEADS,),
        in_specs=[pl.BlockSpec(memory_space=pltpu.SMEM)],
        out_specs=pl.BlockSpec((3, 1, NBR_KEYS, LANE_TILE), lambda h: (0, h, 0, 0)),
        out_shape=jax.ShapeDtypeStruct((3, N_HEADS, NBR_KEYS, LANE_TILE), F32),
        compiler_params=_params(("parallel",)),
        name="nbr_bias",
    )(rpb.astype(F32).reshape(-1))


def _nbr_attn(q_pad, k_nat, vT, bias, sgT, sg_row0, batch, seq):
    nt = seq // TQ
    sg_blk0 = sg_row0 // (2 * HEAD_DIM)
    bias_block = (1, 2, NBR_KEYS, LANE_TILE)
    return pl.pallas_call(
        functools.partial(_nbr_kernel, seq=seq),
        grid=(batch, N_HEADS // 2, nt),
        in_specs=[
            pl.BlockSpec((2, KV_PAD, TQ), lambda b, g, i: (g, 0, b * nt + i)),
            pl.BlockSpec((seq, KV_PAD), lambda b, g, i: (b, g)),
            pl.BlockSpec((2 * HEAD_DIM, seq), lambda b, g, i: (g, b)),
            pl.BlockSpec(bias_block, lambda b, g, i: (jnp.where(i == 0, 0, 1), g, 0, 0)),
            pl.BlockSpec(bias_block, lambda b, g, i: (jnp.where(i == nt - 1, 2, 1), g, 0, 0)),
            pl.BlockSpec((2 * HEAD_DIM, TQ), lambda b, g, i: (sg_blk0 + g, b * nt + i)),
        ],
        out_specs=pl.BlockSpec((2 * HEAD_DIM, TQ), lambda b, g, i: (g, b * nt + i)),
        out_shape=jax.ShapeDtypeStruct((BRANCH_W, batch * seq), BF16),
        compiler_params=_params(("parallel", "parallel", "parallel")),
        name="nbr_attn",
    )(q_pad, k_nat, vT, bias, bias, sgT)


def _out_proj_kernel(x_ref, ma_ref, mb_ref, mc_ref, md_ref, w_ref, *rest, token_major, final):
    acc = x_ref[...].T if token_major else x_ref[...]
    for j, m_ref in enumerate((ma_ref, mb_ref, mc_ref, md_ref)):
        acc = acc + jnp.dot(w_ref[:, BRANCH_W * j:BRANCH_W * (j + 1)], m_ref[...],
                            preferred_element_type=F32)
    if final:
        fg_ref, o_ref = rest
        ms = jnp.mean(acc * acc, axis=0, keepdims=True)
        o_ref[...] = (acc * lax.rsqrt(ms + EPS) * fg_ref[...]).T
    else:
        (o_ref,) = rest
        o_ref[...] = acc


def _out_proj(x, mixes, w_outT, token_major, final_g_col=None):
    t, d = x.shape if token_major else x.shape[::-1]
    tm = TM_OUT_PROJ
    final = final_g_col is not None
    chan = lambda w: pl.BlockSpec((w, tm), lambda i: (0, i))
    nat = pl.BlockSpec((tm, d), lambda i: (i, 0))
    in_specs = ([nat if token_major else chan(d)] + [chan(BRANCH_W)] * 4
                + [pl.BlockSpec((d, 4 * BRANCH_W), lambda i: (0, 0))])
    args = [x, *mixes, w_outT]
    if final:
        in_specs.append(pl.BlockSpec((d, 1), lambda i: (0, 0)))
        args.append(final_g_col)
        out_spec = nat
        out_shape = jax.ShapeDtypeStruct((t, d), F32)
    else:
        out_spec = chan(d)
        out_shape = jax.ShapeDtypeStruct((d, t), F32)
    return pl.pallas_call(
        functools.partial(_out_proj_kernel, token_major=token_major, final=final),
        grid=(t // tm,),
        in_specs=in_specs,
        out_specs=out_spec,
        out_shape=out_shape,
        compiler_params=_params(("parallel",)),
        name="out_proj_final" if final else "out_proj",
    )(*args)


def _rope_tables(seq):
    t = jnp.arange(seq)
    row, col = t // GRID_W, t % GRID_W

    def cs(pos, d):
        inv = jnp.power(ROPE_THETA, -jnp.arange(0, d, 2, dtype=F32) / d)
        ang = pos.astype(F32)[None, :] * inv[:, None]
        return jnp.cos(ang), jnp.sin(ang)

    cr, sr = cs(row, HEAD_DIM // 2)
    cc, sc = cs(col, HEAD_DIM // 2)
    ct, st = cs(t, HEAD_DIM)
    cdt, sdt = cs(t, DIFF_DIM)
    cat = lambda *a: jnp.concatenate(a, axis=0)
    return (cat(cr, cr, cc, cc), cat(-sr, sr, -sc, sc),
            cat(ct, ct), cat(-st, st),
            cat(cdt, cdt, cdt, cdt), cat(-sdt, sdt, -sdt, sdt))


def kernel(x, norm_g, w_in, w_out, qn_a, kn_a, sink_b, rpb_c, lam_q1, lam_k1, lam_q2, lam_k2, subln_d, final_g):
    batch, seq, d = x.shape
    depth = w_in.shape[0]
    tables = _rope_tables(seq)
    res = x.reshape(batch * seq, d)
    out = None
    for l in range(depth):
        token_major = l == 0
        w_inT = w_in[l].T.astype(BF16)
        w_outT = w_out[l].T.astype(BF16)
        (qa, ka, va, qb, kb, vb, qc, kc, vc, qd, kd, vd, sg) = _in_proj(
            res, norm_g[l].reshape(d, 1), w_inT, qn_a[l].reshape(HEAD_DIM, 1), kn_a[l].reshape(HEAD_DIM, 1),
            tables, seq, token_major)
        mix_a = _dense_attn(qa, ka, va, sg, 0, batch, seq, diff=False)
        mix_b = _window_attn(sink_b[l], qb, kb, vb, sg, BRANCH_W, batch, seq)
        mix_c = _nbr_attn(qc, kc, vc, _nbr_bias_tables(rpb_c[l], seq), sg, 2 * BRANCH_W, batch, seq)
        lam_init = 0.8 - 0.6 * math.exp(-0.3 * l)
        lam_vecs = jnp.stack([lam_q1[l], lam_k1[l], lam_q2[l], lam_k2[l]]).astype(F32)
        mix_d = _dense_attn(qd, kd, vd, sg, 3 * BRANCH_W, batch, seq, diff=True, lam_vecs=lam_vecs,
                            subln_col=subln_d[l].reshape(HEAD_DIM, 1), lam_init=lam_init)
        mixes = (mix_a, mix_b, mix_c, mix_d)
        if l == depth - 1:
            out = _out_proj(res, mixes, w_outT, token_major, final_g.reshape(d, 1))
        else:
            res = _out_proj(res, mixes, w_outT, token_major)
    return out.reshape(batch, seq, d)
```

```python
import functools
import math

import jax
import jax.numpy as jnp
import numpy as np
from jax import lax
from jax.experimental import pallas as pl
from jax.experimental.pallas import tpu as pltpu

HEAD_DIM = 64
GRID_W = 64
DIFF_DIM = 32
WINDOW = 128
NA_KH = 8
NA_KW = 16
ROPE_THETA = 10000.0
EPS = 1e-6
NEG_INF = -1e30
LOG2E = math.log2(math.e)

N_HEADS = 4
BRANCH_W = N_HEADS * HEAD_DIM
KV_PAD = 128
ONES_ROWS = 16
V_EXT = HEAD_DIM + ONES_ROWS

V7X_VMEM_BYTES = 64 * 1024 * 1024
VMEM_LIMIT = V7X_VMEM_BYTES * 3 // 4

TM_PROJ = 512
TM_OUT_PROJ = 512
TQ = 512
TK = 512
CHUNKS_PER_TRIP = 8
LANE_TILE = 256
WIN_KEYS = LANE_TILE + 2 * WINDOW
NBR_TILE_ROWS = LANE_TILE // GRID_W
NBR_KEY_ROWS = NBR_TILE_ROWS + NA_KH
NBR_KEYS = NBR_KEY_ROWS * GRID_W

F32 = jnp.float32
BF16 = jnp.bfloat16


def _params(sem):
    return pltpu.CompilerParams(dimension_semantics=sem, vmem_limit_bytes=VMEM_LIMIT)


def _rot_quarters(y):
    return jnp.concatenate([y[16:32], y[0:16], y[48:64], y[32:48]], axis=0)


def _rot_halves(y):
    return jnp.concatenate([y[32:64], y[0:32]], axis=0)


def _head_rms(y, g_col):
    ms = jnp.mean(y * y, axis=0, keepdims=True)
    return y * lax.rsqrt(ms + EPS) * g_col


def _pad_rows(y, slot, n_slots):
    z = jnp.zeros_like(y)
    return jnp.concatenate([y if s == slot else z for s in range(n_slots)], axis=0)


def _silu(g):
    return g * (1.0 / (1.0 + jnp.exp(-g)))


def _in_proj_kernel(x_ref, ng_ref, w_ref, qn_ref, kn_ref,
                    ca_ref, sa_ref, cb_ref, sb_ref, cd_ref, sd_ref,
                    qa_ref, ka_ref, va_ref, qb_ref, kb_ref, vb_ref,
                    qc_ref, kc_ref, vc_ref, qd_ref, kd_ref, vd_ref, sg_ref,
                    h_ref, *, token_major):
    x = x_ref[...].T if token_major else x_ref[...]
    ms = jnp.mean(x * x, axis=0, keepdims=True)
    h_ref[...] = (x * lax.rsqrt(ms + EPS) * ng_ref[...]).astype(BF16)

    def proj(r0, nrows):
        return jnp.dot(w_ref[r0:r0 + nrows, :], h_ref[...], preferred_element_type=F32)

    sc_hd = HEAD_DIM ** -0.5
    ca, sa = ca_ref[...], sa_ref[...]
    cb, sb = cb_ref[...], sb_ref[...]
    cd, sd = cd_ref[...], sd_ref[...]
    qn, kn = qn_ref[...], kn_ref[...]

    def rope_a(y):
        return y * ca + _rot_quarters(y) * sa

    def rope_b(y):
        return y * cb + _rot_halves(y) * sb

    def rope_d(y):
        return y * cd + _rot_quarters(y) * sd

    base = 0
    z = proj(base, 256)
    for hh in range(N_HEADS):
        y = rope_a(_head_rms(z[64 * hh:64 * hh + 64], qn)) * (sc_hd * LOG2E)
        qa_ref[hh] = _pad_rows(y, hh // 2, 2).astype(BF16)
    z = proj(base + 256, 128)
    kk = jnp.concatenate([rope_a(_head_rms(z[64 * j:64 * j + 64], kn)) for j in range(2)], axis=0)
    ka_ref[...] = kk.T.astype(BF16)
    va_ref[...] = proj(base + 384, 128).astype(BF16)
    sg_ref[0:256, :] = _silu(proj(base + 512, 256)).astype(BF16)

    base = 768
    z = proj(base, 256)
    for hh in range(N_HEADS):
        y = rope_b(z[64 * hh:64 * hh + 64]) * (sc_hd * LOG2E)
        qb_ref[hh] = _pad_rows(y, hh // 2, 2).astype(BF16)
    z = proj(base + 256, 128)
    kk = jnp.concatenate([rope_b(z[64 * j:64 * j + 64]) for j in range(2)], axis=0)
    kb_ref[...] = kk.T.astype(BF16)
    vb_ref[...] = proj(base + 384, 128).astype(BF16)
    sg_ref[256:512, :] = _silu(proj(base + 512, 256)).astype(BF16)

    base = 1536
    z = proj(base, 256)
    for hh in range(N_HEADS):
        y = z[64 * hh:64 * hh + 64] * (sc_hd * LOG2E)
        qc_ref[hh] = _pad_rows(y, hh % 2, 2).astype(BF16)
    z = proj(base + 256, 256)
    kc_ref[...] = z.T.astype(BF16)
    vc_ref[...] = proj(base + 512, 256).astype(BF16)
    sg_ref[512:768, :] = _silu(proj(base + 768, 256)).astype(BF16)

    base = 2560
    z = proj(base, 256)
    sc_dd = DIFF_DIM ** -0.5
    for hh in range(N_HEADS):
        y = rope_d(z[64 * hh:64 * hh + 64]) * (sc_dd * LOG2E)
        for c in range(2):
            qd_ref[2 * hh + c] = _pad_rows(y[32 * c:32 * c + 32], 2 * (hh // 2) + c, 4).astype(BF16)
    z = proj(base + 256, 128)
    kk = jnp.concatenate([rope_d(z[64 * j:64 * j + 64]) for j in range(2)], axis=0)
    kd_ref[...] = kk.T.astype(BF16)
    sg_ref[768:1024, :] = _silu(proj(base + 512, 256)).astype(BF16)
    vd_ref[...] = proj(base + 384, 128).astype(BF16)


def _in_proj(x, ng_col, w_inT, qn_col, kn_col, tables, seq, token_major):
    t, d = x.shape if token_major else x.shape[::-1]
    tm = TM_PROJ
    nt_seq = seq // tm
    d_in = w_inT.shape[0]
    full = lambda shape: pl.BlockSpec(shape, lambda i: (0,) * len(shape))
    tab = pl.BlockSpec((HEAD_DIM, tm), lambda i: (0, i % nt_seq))
    qpad = lambda n: pl.BlockSpec((n, KV_PAD, tm), lambda i: (0, 0, i))
    nat = lambda w: pl.BlockSpec((tm, w), lambda i: (i, 0))
    chan = lambda w: pl.BlockSpec((w, tm), lambda i: (0, i))
    sds = jax.ShapeDtypeStruct
    out_shape = [
        sds((N_HEADS, KV_PAD, t), BF16), sds((t, 128), BF16), sds((128, t), BF16),
        sds((N_HEADS, KV_PAD, t), BF16), sds((t, 128), BF16), sds((128, t), BF16),
        sds((N_HEADS, KV_PAD, t), BF16), sds((t, 256), BF16), sds((256, t), BF16),
        sds((2 * N_HEADS, KV_PAD, t), BF16), sds((t, 128), BF16), sds((128, t), BF16),
        sds((4 * BRANCH_W, t), BF16),
    ]
    out_specs = [
        qpad(N_HEADS), nat(128), chan(128),
        qpad(N_HEADS), nat(128), chan(128),
        qpad(N_HEADS), nat(256), chan(256),
        qpad(2 * N_HEADS), nat(128), chan(128),
        chan(4 * BRANCH_W),
    ]
    return pl.pallas_call(
        functools.partial(_in_proj_kernel, token_major=token_major),
        grid=(t // tm,),
        in_specs=[nat(d) if token_major else chan(d), full((d, 1)), full((d_in, d)),
                  full((HEAD_DIM, 1)), full((HEAD_DIM, 1)), tab, tab, tab, tab, tab, tab],
        out_specs=out_specs,
        out_shape=out_shape,
        scratch_shapes=[pltpu.VMEM((d, tm), BF16)],
        compiler_params=_params(("parallel",)),
        name="in_proj",
    )(x, ng_col, w_inT, qn_col, kn_col, *tables)


def _v_ext(v):
    return jnp.concatenate([v, jnp.ones((ONES_ROWS, v.shape[1]), v.dtype)], axis=0)


def _dense_kernel(*refs, n_q, diff, lam_init, seq):
    if diff:
        (q_ref, k_ref, v_ref, sg_ref, lam_ref, sub_ref, o_ref, s0_ref, s1_ref, acc_ref) = refs
    else:
        (q_ref, k_ref, v_ref, sg_ref, o_ref, s0_ref, s1_ref, acc_ref) = refs
    s_slots = (s0_ref, s1_ref)
    n_chunks = seq // TK

    acc_ref[...] = jnp.zeros(acc_ref.shape, F32)

    tiles = [(i, slice(LANE_TILE * h, LANE_TILE * (h + 1))) for i in range(n_q) for h in range(TQ // LANE_TILE)]

    def score_tile(slot, chunk, i, lanes):
        k = k_ref[pl.ds(pl.multiple_of(chunk * TK, TK), TK), :]
        s = jnp.dot(k, q_ref[i, :, lanes], preferred_element_type=F32).astype(BF16)
        s_slots[slot][i, :, lanes] = s
        return jnp.max(s, axis=0, keepdims=True).astype(F32)

    def value_tile(slot, chunk, i, lanes, m_old, m_new):
        v = _v_ext(v_ref[:, pl.ds(pl.multiple_of(chunk * TK, TK), TK)])
        alpha = jnp.exp2(m_old - m_new)
        p = jnp.exp2(s_slots[slot][i, :, lanes] - m_new.astype(BF16))
        acc_ref[i, :, lanes] = alpha * acc_ref[i, :, lanes] + jnp.dot(v, p, preferred_element_type=F32)

    def body(jj, carry):
        m_old, m_run = carry
        c0 = CHUNKS_PER_TRIP * jj
        for u in range(CHUNKS_PER_TRIP):
            c_next = jnp.minimum(c0 + u + 1, n_chunks - 1)
            nxt = []
            for t, (i, lanes) in enumerate(tiles):
                nxt.append(score_tile((u + 1) % 2, c_next, i, lanes))
                value_tile(u % 2, c0 + u, i, lanes, m_old[t], m_run[t])
            m_old, m_run = m_run, [jnp.maximum(a, b) for a, b in zip(m_run, nxt)]
        return m_old, m_run

    first = [score_tile(0, 0, i, lanes) for i, lanes in tiles]
    minus_inf = [jnp.full((1, LANE_TILE), -jnp.inf, F32)] * len(tiles)
    lax.fori_loop(0, n_chunks // CHUNKS_PER_TRIP, body, (minus_inf, first))

    def normalized(i):
        acc = acc_ref[i]
        return acc[0:HEAD_DIM] / acc[HEAD_DIM:HEAD_DIM + 1]

    if diff:
        lv = lam_ref[...]
        lam = (jnp.exp(jnp.sum(lv[0:1] * lv[1:2], axis=1, keepdims=True))
               - jnp.exp(jnp.sum(lv[2:3] * lv[3:4], axis=1, keepdims=True)) + lam_init)
        o = normalized(0) - lam * normalized(1)
        ms = jnp.mean(o * o, axis=0, keepdims=True)
        o = o * lax.rsqrt(ms + EPS) * sub_ref[...] * (1.0 - lam_init)
        o_ref[...] = (o * sg_ref[...].astype(F32)).astype(BF16)
    else:
        for i in range(n_q):
            r = slice(HEAD_DIM * i, HEAD_DIM * (i + 1))
            o_ref[r, :] = (normalized(i) * sg_ref[r, :].astype(F32)).astype(BF16)


def _dense_attn(q_pad, k_nat, vT, sgT, sg_row0, batch, seq, diff, lam_vecs=None, subln_col=None,
                lam_init=0.0):
    nt = seq // TQ
    n_q = 2
    out_rows = HEAD_DIM if diff else 2 * HEAD_DIM
    n_groups = BRANCH_W // out_rows
    sg_blk0 = sg_row0 // out_rows
    in_specs = [
        pl.BlockSpec((n_q, KV_PAD, TQ), lambda b, g, i: (g, 0, b * nt + i)),
        pl.BlockSpec((seq, KV_PAD), lambda b, g, i: (b, 0)),
        pl.BlockSpec((HEAD_DIM, seq), (lambda b, g, i: (g // 2, b)) if diff else (lambda b, g, i: (g, b))),
        pl.BlockSpec((out_rows, TQ), lambda b, g, i: (sg_blk0 + g, b * nt + i)),
    ]
    args = [q_pad, k_nat, vT, sgT]
    if diff:
        in_specs += [pl.BlockSpec((4, DIFF_DIM), lambda b, g, i: (0, 0)),
                     pl.BlockSpec((HEAD_DIM, 1), lambda b, g, i: (0, 0))]
        args += [lam_vecs, subln_col]
    return pl.pallas_call(
        functools.partial(_dense_kernel, n_q=n_q, diff=diff, lam_init=lam_init, seq=seq),
        grid=(batch, n_groups, nt),
        in_specs=in_specs,
        out_specs=pl.BlockSpec((out_rows, TQ), lambda b, g, i: (g, b * nt + i)),
        out_shape=jax.ShapeDtypeStruct((BRANCH_W, batch * seq), BF16),
        scratch_shapes=[pltpu.VMEM((n_q, TK, TQ), BF16), pltpu.VMEM((n_q, TK, TQ), BF16),
                        pltpu.VMEM((n_q, V_EXT, TQ), F32)],
        compiler_params=_params(("parallel", "parallel", "parallel")),
        name="diff_attn" if diff else "dense_attn",
    )(*args)


def _softmax_value_tile(sb, v, extra_logit=None):
    m = jnp.max(sb, axis=0, keepdims=True)
    if extra_logit is not None:
        m = jnp.maximum(m.astype(F32), extra_logit).astype(BF16)
    p = jnp.exp2(sb - m)
    acc = jnp.dot(v, p, preferred_element_type=F32)
    denom = acc[HEAD_DIM:HEAD_DIM + 1]
    if extra_logit is not None:
        denom = denom + jnp.exp2(extra_logit - m.astype(F32))
    return acc[0:HEAD_DIM], denom


def _window_kernel(sink_ref, q_ref, k_ref, v_ref, sg_ref, o_ref, *, seq):
    g = pl.program_id(1)
    q0 = pl.program_id(2) * TQ
    shape = (WIN_KEYS, LANE_TILE)
    rel = lax.broadcasted_iota(jnp.int32, shape, 0) - lax.broadcasted_iota(jnp.int32, shape, 1)
    pending = []
    for h in range(TQ // LANE_TILE):
        lanes = slice(LANE_TILE * h, LANE_TILE * (h + 1))
        qs = q0 + LANE_TILE * h
        start = pl.multiple_of(jnp.clip(qs - WINDOW, 0, seq - WIN_KEYS), WINDOW)
        k = k_ref[pl.ds(start, WIN_KEYS), :]
        inside = jnp.abs(rel + (start - qs)) <= WINDOW
        for i in range(2):
            s = jnp.dot(k, q_ref[i, :, lanes], preferred_element_type=F32)
            pending.append((i, lanes, start, jnp.where(inside, s, NEG_INF).astype(BF16)))
    for i, lanes, start, sb in pending:
        rows = slice(HEAD_DIM * i, HEAD_DIM * (i + 1))
        v = _v_ext(v_ref[:, pl.ds(start, WIN_KEYS)])
        out, denom = _softmax_value_tile(sb, v, sink_ref[2 * g + i] * LOG2E)
        o_ref[rows, lanes] = (out / denom * sg_ref[rows, lanes].astype(F32)).astype(BF16)


def _window_attn(sink, q_pad, k_nat, vT, sgT, sg_row0, batch, seq):
    nt = seq // TQ
    sg_blk0 = sg_row0 // (2 * HEAD_DIM)
    return pl.pallas_call(
        functools.partial(_window_kernel, seq=seq),
        grid=(batch, N_HEADS // 2, nt),
        in_specs=[
            pl.BlockSpec(memory_space=pltpu.SMEM),
            pl.BlockSpec((2, KV_PAD, TQ), lambda b, g, i: (g, 0, b * nt + i)),
            pl.BlockSpec((seq, KV_PAD), lambda b, g, i: (b, 0)),
            pl.BlockSpec((HEAD_DIM, seq), lambda b, g, i: (g, b)),
            pl.BlockSpec((2 * HEAD_DIM, TQ), lambda b, g, i: (sg_blk0 + g, b * nt + i)),
        ],
        out_specs=pl.BlockSpec((2 * HEAD_DIM, TQ), lambda b, g, i: (g, b * nt + i)),
        out_shape=jax.ShapeDtypeStruct((BRANCH_W, batch * seq), BF16),
        compiler_params=_params(("parallel", "parallel", "parallel")),
        name="window_attn",
    )(sink, q_pad, k_nat, vT, sgT)


N_DR = 2 * NA_KH - 1
N_DC = 2 * NA_KW - 1


def _nbr_start_row(first_row, rows, clip):
    return clip(first_row - NA_KH // 2, 0, rows - NBR_KEY_ROWS)


def _nbr_kernel(q_ref, k_ref, v_ref, bias_lo_ref, bias_hi_ref, sg_ref, o_ref, *, seq):
    rows = seq // GRID_W
    r0 = pl.program_id(2) * (TQ // GRID_W)
    pending = []
    for h, bias_ref in enumerate((bias_lo_ref, bias_hi_ref)):
        lanes = slice(LANE_TILE * h, LANE_TILE * (h + 1))
        start = _nbr_start_row(r0 + NBR_TILE_ROWS * h, rows, jnp.clip) * GRID_W
        start = pl.multiple_of(start, NBR_TILE_ROWS * GRID_W)
        k = k_ref[pl.ds(start, NBR_KEYS), :]
        for i in range(2):
            s = jnp.dot(k, q_ref[i, :, lanes], preferred_element_type=F32) + bias_ref[0, i]
            pending.append((i, lanes, start, s.astype(BF16)))
    for i, lanes, start, sb in pending:
        hrows = slice(HEAD_DIM * i, HEAD_DIM * (i + 1))
        v = _v_ext(v_ref[hrows, pl.ds(start, NBR_KEYS)])
        out, denom = _softmax_value_tile(sb, v)
        o_ref[hrows, lanes] = (out / denom * sg_ref[hrows, lanes].astype(F32)).astype(BF16)


def _nbr_bias_kernel(rpb_ref, o_ref, *, seq):
    h = pl.program_id(0)
    rows = seq // GRID_W
    kh = min(NA_KH, rows)
    shape = (GRID_W, 2 * GRID_W)
    kc = lax.broadcasted_iota(jnp.int32, shape, 0)
    lane = lax.broadcasted_iota(jnp.int32, shape, 1)
    qc = lane % GRID_W
    dcol = jnp.clip(kc - qc, -(NA_KW - 1), NA_KW - 1) + NA_KW - 1
    cs = jnp.clip(qc - NA_KW // 2, 0, GRID_W - NA_KW)
    col_ok = (kc >= cs) & (kc < cs + NA_KW)
    hits = [dcol == j for j in range(N_DC)]
    toeplitz = []
    for dr in range(N_DR):
        t = jnp.zeros(shape, F32)
        for j in range(N_DC):
            t = jnp.where(hits[j], rpb_ref[(h * N_DR + dr) * N_DC + j], t)
        toeplitz.append(jnp.where(col_ok, t * LOG2E, NEG_INF))
    masked = jnp.full(shape, NEG_INF, F32)
    left = lane < GRID_W
    for vi, first_row in enumerate((0, NBR_TILE_ROWS, rows - NBR_TILE_ROWS)):
        start_row = int(_nbr_start_row(first_row, rows, np.clip))
        for kr in range(NBR_KEY_ROWS):
            krow = start_row + kr
            for qp in range(NBR_TILE_ROWS // 2):
                halves = []
                for qrow in (first_row + 2 * qp, first_row + 2 * qp + 1):
                    rs = int(np.clip(qrow - kh // 2, 0, rows - kh))
                    halves.append(toeplitz[krow - qrow + NA_KH - 1] if rs <= krow < rs + kh else masked)
                blk = halves[0] if halves[0] is halves[1] else jnp.where(left, halves[0], halves[1])
                o_ref[vi, 0, GRID_W * kr:GRID_W * (kr + 1), 2 * GRID_W * qp:2 * GRID_W * (qp + 1)] = blk


def _nbr_bias_tables(rpb, seq):
    return pl.pallas_call(
        functools.partial(_nbr_bias_kernel, seq=seq),
        grid=(N_HEADS,),
        in_specs=[pl.BlockSpec(memory_space=pltpu.SMEM)],
        out_specs=pl.BlockSpec((3, 1, NBR_KEYS, LANE_TILE), lambda h: (0, h, 0, 0)),
        out_shape=jax.ShapeDtypeStruct((3, N_HEADS, NBR_KEYS, LANE_TILE), F32),
        compiler_params=_params(("parallel",)),
        name="nbr_bias",
    )(rpb.astype(F32).reshape(-1))


def _nbr_attn(q_pad, k_nat, vT, bias, sgT, sg_row0, batch, seq):
    nt = seq // TQ
    sg_blk0 = sg_row0 // (2 * HEAD_DIM)
    bias_block = (1, 2, NBR_KEYS, LANE_TILE)
    return pl.pallas_call(
        functools.partial(_nbr_kernel, seq=seq),
        grid=(batch, N_HEADS // 2, nt),
        in_specs=[
            pl.BlockSpec((2, KV_PAD, TQ), lambda b, g, i: (g, 0, b * nt + i)),
            pl.BlockSpec((seq, KV_PAD), lambda b, g, i: (b, g)),
            pl.BlockSpec((2 * HEAD_DIM, seq), lambda b, g, i: (g, b)),
            pl.BlockSpec(bias_block, lambda b, g, i: (jnp.where(i == 0, 0, 1), g, 0, 0)),
            pl.BlockSpec(bias_block, lambda b, g, i: (jnp.where(i == nt - 1, 2, 1), g, 0, 0)),
            pl.BlockSpec((2 * HEAD_DIM, TQ), lambda b, g, i: (sg_blk0 + g, b * nt + i)),
        ],
        out_specs=pl.BlockSpec((2 * HEAD_DIM, TQ), lambda b, g, i: (g, b * nt + i)),
        out_shape=jax.ShapeDtypeStruct((BRANCH_W, batch * seq), BF16),
        compiler_params=_params(("parallel", "parallel", "parallel")),
        name="nbr_attn",
    )(q_pad, k_nat, vT, bias, bias, sgT)


def _out_proj_kernel(x_ref, ma_ref, mb_ref, mc_ref, md_ref, w_ref, *rest, token_major, final):
    acc = x_ref[...].T if token_major else x_ref[...]
    for j, m_ref in enumerate((ma_ref, mb_ref, mc_ref, md_ref)):
        acc = acc + jnp.dot(w_ref[:, BRANCH_W * j:BRANCH_W * (j + 1)], m_ref[...],
                            preferred_element_type=F32)
    if final:
        fg_ref, o_ref = rest
        ms = jnp.mean(acc * acc, axis=0, keepdims=True)
        o_ref[...] = (acc * lax.rsqrt(ms + EPS) * fg_ref[...]).T
    else:
        (o_ref,) = rest
        o_ref[...] = acc


def _out_proj(x, mixes, w_outT, token_major, final_g_col=None):
    t, d = x.shape if token_major else x.shape[::-1]
    tm = TM_OUT_PROJ
    final = final_g_col is not None
    chan = lambda w: pl.BlockSpec((w, tm), lambda i: (0, i))
    nat = pl.BlockSpec((tm, d), lambda i: (i, 0))
    in_specs = ([nat if token_major else chan(d)] + [chan(BRANCH_W)] * 4
                + [pl.BlockSpec((d, 4 * BRANCH_W), lambda i: (0, 0))])
    args = [x, *mixes, w_outT]
    if final:
        in_specs.append(pl.BlockSpec((d, 1), lambda i: (0, 0)))
        args.append(final_g_col)
        out_spec = nat
        out_shape = jax.ShapeDtypeStruct((t, d), F32)
    else:
        out_spec = chan(d)
        out_shape = jax.ShapeDtypeStruct((d, t), F32)
    return pl.pallas_call(
        functools.partial(_out_proj_kernel, token_major=token_major, final=final),
        grid=(t // tm,),
        in_specs=in_specs,
        out_specs=out_spec,
        out_shape=out_shape,
        compiler_params=_params(("parallel",)),
        name="out_proj_final" if final else "out_proj",
    )(*args)


def _rope_tables(seq):
    t = jnp.arange(seq)
    row, col = t // GRID_W, t % GRID_W

    def cs(pos, d):
        inv = jnp.power(ROPE_THETA, -jnp.arange(0, d, 2, dtype=F32) / d)
        ang = pos.astype(F32)[None, :] * inv[:, None]
        return jnp.cos(ang), jnp.sin(ang)

    cr, sr = cs(row, HEAD_DIM // 2)
    cc, sc = cs(col, HEAD_DIM // 2)
    ct, st = cs(t, HEAD_DIM)
    cdt, sdt = cs(t, DIFF_DIM)
    cat = lambda *a: jnp.concatenate(a, axis=0)
    return (cat(cr, cr, cc, cc), cat(-sr, sr, -sc, sc),
            cat(ct, ct), cat(-st, st),
            cat(cdt, cdt, cdt, cdt), cat(-sdt, sdt, -sdt, sdt))


def kernel(x, norm_g, w_in, w_out, qn_a, kn_a, sink_b, rpb_c, lam_q1, lam_k1, lam_q2, lam_k2, subln_d, final_g):
    batch, seq, d = x.shape
    depth = w_in.shape[0]
    tables = _rope_tables(seq)
    res = x.reshape(batch * seq, d)
    out = None
    for l in range(depth):
        token_major = l == 0
        w_inT = w_in[l].T.astype(BF16)
        w_outT = w_out[l].T.astype(BF16)
        (qa, ka, va, qb, kb, vb, qc, kc, vc, qd, kd, vd, sg) = _in_proj(
            res, norm_g[l].reshape(d, 1), w_inT, qn_a[l].reshape(HEAD_DIM, 1), kn_a[l].reshape(HEAD_DIM, 1),
            tables, seq, token_major)
        mix_a = _dense_attn(qa, ka, va, sg, 0, batch, seq, diff=False)
        mix_b = _window_attn(sink_b[l], qb, kb, vb, sg, BRANCH_W, batch, seq)
        mix_c = _nbr_attn(qc, kc, vc, _nbr_bias_tables(rpb_c[l], seq), sg, 2 * BRANCH_W, batch, seq)
        lam_init = 0.8 - 0.6 * math.exp(-0.3 * l)
        lam_vecs = jnp.stack([lam_q1[l], lam_k1[l], lam_q2[l], lam_k2[l]]).astype(F32)
        mix_d = _dense_attn(qd, kd, vd, sg, 3 * BRANCH_W, batch, seq, diff=True, lam_vecs=lam_vecs,
                            subln_col=subln_d[l].reshape(HEAD_DIM, 1), lam_init=lam_init)
        mixes = (mix_a, mix_b, mix_c, mix_d)
        if l == depth - 1:
            out = _out_proj(res, mixes, w_outT, token_major, final_g.reshape(d, 1))
        else:
            res = _out_proj(res, mixes, w_outT, token_major)
    return out.reshape(batch, seq, d)
```

```python
import functools
import math

import jax
import jax.numpy as jnp
import numpy as np
from jax import lax
from jax.experimental import pallas as pl
from jax.experimental.pallas import tpu as pltpu

HEAD_DIM = 64
GRID_W = 64
DIFF_DIM = 32
WINDOW = 128
NA_KH = 8
NA_KW = 16
ROPE_THETA = 10000.0
EPS = 1e-6
NEG_INF = -1e30
LOG2E = math.log2(math.e)

N_HEADS = 4
BRANCH_W = N_HEADS * HEAD_DIM
KV_PAD = 128
ONES_ROWS = 16
V_EXT = HEAD_DIM + ONES_ROWS

V7X_VMEM_BYTES = 64 * 1024 * 1024
VMEM_LIMIT = V7X_VMEM_BYTES * 3 // 4

TM_PROJ = 512
TM_OUT_PROJ = 512
TQ = 1024
TQ_DENSE = 512
TK = 512
CHUNKS_PER_TRIP = 8
LANE_TILE = 256
WIN_KEYS = LANE_TILE + 2 * WINDOW
NBR_TILE_ROWS = LANE_TILE // GRID_W
NBR_KEY_ROWS = NBR_TILE_ROWS + NA_KH
NBR_KEYS = NBR_KEY_ROWS * GRID_W

F32 = jnp.float32
BF16 = jnp.bfloat16


def _params(sem):
    return pltpu.CompilerParams(dimension_semantics=sem, vmem_limit_bytes=VMEM_LIMIT)


def _rot_quarters(y):
    return jnp.concatenate([y[16:32], y[0:16], y[48:64], y[32:48]], axis=0)


def _rot_halves(y):
    return jnp.concatenate([y[32:64], y[0:32]], axis=0)


def _head_rms(y, g_col):
    ms = jnp.mean(y * y, axis=0, keepdims=True)
    return y * lax.rsqrt(ms + EPS) * g_col


def _pad_rows(y, slot, n_slots):
    z = jnp.zeros_like(y)
    return jnp.concatenate([y if s == slot else z for s in range(n_slots)], axis=0)


def _silu(g):
    return g * (1.0 / (1.0 + jnp.exp(-g)))


def _in_proj_kernel(x_ref, ng_ref, w_ref, qn_ref, kn_ref,
                    ca_ref, sa_ref, cb_ref, sb_ref, cd_ref, sd_ref,
                    qa_ref, ka_ref, va_ref, qb_ref, kb_ref, vb_ref,
                    qc_ref, kc_ref, vc_ref, qd_ref, kd_ref, vd_ref, sg_ref,
                    h_ref, *, token_major):
    x = x_ref[...].T if token_major else x_ref[...]
    ms = jnp.mean(x * x, axis=0, keepdims=True)
    h_ref[...] = (x * lax.rsqrt(ms + EPS) * ng_ref[...]).astype(BF16)

    def proj(r0, nrows):
        return jnp.dot(w_ref[r0:r0 + nrows, :], h_ref[...], preferred_element_type=F32)

    sc_hd = HEAD_DIM ** -0.5
    ca, sa = ca_ref[...], sa_ref[...]
    cb, sb = cb_ref[...], sb_ref[...]
    cd, sd = cd_ref[...], sd_ref[...]
    qn, kn = qn_ref[...], kn_ref[...]

    def rope_a(y):
        return y * ca + _rot_quarters(y) * sa

    def rope_b(y):
        return y * cb + _rot_halves(y) * sb

    def rope_d(y):
        return y * cd + _rot_quarters(y) * sd

    base = 0
    z = proj(base, 256)
    for hh in range(N_HEADS):
        y = rope_a(_head_rms(z[64 * hh:64 * hh + 64], qn)) * (sc_hd * LOG2E)
        qa_ref[hh] = _pad_rows(y, hh // 2, 2).astype(BF16)
    z = proj(base + 256, 128)
    kk = jnp.concatenate([rope_a(_head_rms(z[64 * j:64 * j + 64], kn)) for j in range(2)], axis=0)
    ka_ref[...] = kk.T.astype(BF16)
    va_ref[...] = proj(base + 384, 128).astype(BF16)
    sg_ref[0:256, :] = _silu(proj(base + 512, 256)).astype(BF16)

    base = 768
    z = proj(base, 256)
    for hh in range(N_HEADS):
        y = rope_b(z[64 * hh:64 * hh + 64]) * (sc_hd * LOG2E)
        qb_ref[hh] = _pad_rows(y, hh // 2, 2).astype(BF16)
    z = proj(base + 256, 128)
    kk = jnp.concatenate([rope_b(z[64 * j:64 * j + 64]) for j in range(2)], axis=0)
    kb_ref[...] = kk.T.astype(BF16)
    vb_ref[...] = proj(base + 384, 128).astype(BF16)
    sg_ref[256:512, :] = _silu(proj(base + 512, 256)).astype(BF16)

    base = 1536
    z = proj(base, 256)
    for hh in range(N_HEADS):
        y = z[64 * hh:64 * hh + 64] * (sc_hd * LOG2E)
        qc_ref[hh] = _pad_rows(y, hh % 2, 2).astype(BF16)
    z = proj(base + 256, 256)
    kc_ref[...] = z.T.astype(BF16)
    vc_ref[...] = proj(base + 512, 256).astype(BF16)
    sg_ref[512:768, :] = _silu(proj(base + 768, 256)).astype(BF16)

    base = 2560
    z = proj(base, 256)
    sc_dd = DIFF_DIM ** -0.5
    for hh in range(N_HEADS):
        y = rope_d(z[64 * hh:64 * hh + 64]) * (sc_dd * LOG2E)
        for c in range(2):
            qd_ref[2 * hh + c] = _pad_rows(y[32 * c:32 * c + 32], 2 * (hh // 2) + c, 4).astype(BF16)
    z = proj(base + 256, 128)
    kk = jnp.concatenate([rope_d(z[64 * j:64 * j + 64]) for j in range(2)], axis=0)
    kd_ref[...] = kk.T.astype(BF16)
    sg_ref[768:1024, :] = _silu(proj(base + 512, 256)).astype(BF16)
    vd_ref[...] = proj(base + 384, 128).astype(BF16)


def _in_proj(x, ng_col, w_inT, qn_col, kn_col, tables, seq, token_major):
    t, d = x.shape if token_major else x.shape[::-1]
    tm = TM_PROJ
    nt_seq = seq // tm
    d_in = w_inT.shape[0]
    full = lambda shape: pl.BlockSpec(shape, lambda i: (0,) * len(shape))
    tab = pl.BlockSpec((HEAD_DIM, tm), lambda i: (0, i % nt_seq))
    qpad = lambda n: pl.BlockSpec((n, KV_PAD, tm), lambda i: (0, 0, i))
    nat = lambda w: pl.BlockSpec((tm, w), lambda i: (i, 0))
    chan = lambda w: pl.BlockSpec((w, tm), lambda i: (0, i))
    sds = jax.ShapeDtypeStruct
    out_shape = [
        sds((N_HEADS, KV_PAD, t), BF16), sds((t, 128), BF16), sds((128, t), BF16),
        sds((N_HEADS, KV_PAD, t), BF16), sds((t, 128), BF16), sds((128, t), BF16),
        sds((N_HEADS, KV_PAD, t), BF16), sds((t, 256), BF16), sds((256, t), BF16),
        sds((2 * N_HEADS, KV_PAD, t), BF16), sds((t, 128), BF16), sds((128, t), BF16),
        sds((4 * BRANCH_W, t), BF16),
    ]
    out_specs = [
        qpad(N_HEADS), nat(128), chan(128),
        qpad(N_HEADS), nat(128), chan(128),
        qpad(N_HEADS), nat(256), chan(256),
        qpad(2 * N_HEADS), nat(128), chan(128),
        chan(4 * BRANCH_W),
    ]
    return pl.pallas_call(
        functools.partial(_in_proj_kernel, token_major=token_major),
        grid=(t // tm,),
        in_specs=[nat(d) if token_major else chan(d), full((d, 1)), full((d_in, d)),
                  full((HEAD_DIM, 1)), full((HEAD_DIM, 1)), tab, tab, tab, tab, tab, tab],
        out_specs=out_specs,
        out_shape=out_shape,
        scratch_shapes=[pltpu.VMEM((d, tm), BF16)],
        compiler_params=_params(("parallel",)),
        name="in_proj",
    )(x, ng_col, w_inT, qn_col, kn_col, *tables)


def _v_ext(v):
    return jnp.concatenate([v, jnp.ones((ONES_ROWS, v.shape[1]), v.dtype)], axis=0)


def _dense_kernel(*refs, n_q, diff, lam_init, seq):
    if diff:
        (q_ref, k_ref, v_ref, sg_ref, lam_ref, sub_ref, o_ref, s0_ref, s1_ref, acc_ref) = refs
    else:
        (q_ref, k_ref, v_ref, sg_ref, o_ref, s0_ref, s1_ref, acc_ref) = refs
    s_slots = (s0_ref, s1_ref)
    n_chunks = seq // TK
    n_qtiles = seq // TQ_DENSE
    trips_per_qtile = n_chunks // CHUNKS_PER_TRIP

    acc_ref[...] = jnp.zeros(acc_ref.shape, F32)

    tiles = [(i, slice(LANE_TILE * h, LANE_TILE * (h + 1))) for i in range(n_q) for h in range(TQ_DENSE // LANE_TILE)]

    def score_tile(slot, qtile, chunk, i, lanes):
        k = k_ref[pl.ds(pl.multiple_of(chunk * TK, TK), TK), :]
        q = q_ref[i, :, pl.ds(pl.multiple_of(qtile * TQ_DENSE + lanes.start, LANE_TILE), LANE_TILE)]
        s = jnp.dot(k, q, preferred_element_type=F32).astype(BF16)
        s_slots[slot][i, :, lanes] = s
        return jnp.max(s, axis=0, keepdims=True).astype(F32)

    def value_tile(slot, chunk, i, lanes, m_old, m_new):
        v = _v_ext(v_ref[:, pl.ds(pl.multiple_of(chunk * TK, TK), TK)])
        alpha = jnp.exp2(m_old - m_new)
        p = jnp.exp2(s_slots[slot][i, :, lanes] - m_new.astype(BF16))
        acc_ref[i, :, lanes] = alpha * acc_ref[i, :, lanes] + jnp.dot(v, p, preferred_element_type=F32)

    def normalized(i):
        acc = acc_ref[i]
        return acc[0:HEAD_DIM] / acc[HEAD_DIM:HEAD_DIM + 1]

    def finalize(qtile):
        cols = pl.ds(pl.multiple_of(qtile * TQ_DENSE, TQ_DENSE), TQ_DENSE)
        if diff:
            lv = lam_ref[...]
            lam = (jnp.exp(jnp.sum(lv[0:1] * lv[1:2], axis=1, keepdims=True))
                   - jnp.exp(jnp.sum(lv[2:3] * lv[3:4], axis=1, keepdims=True)) + lam_init)
            o = normalized(0) - lam * normalized(1)
            ms = jnp.mean(o * o, axis=0, keepdims=True)
            o = o * lax.rsqrt(ms + EPS) * sub_ref[...] * (1.0 - lam_init)
            o_ref[:, cols] = (o * sg_ref[:, cols].astype(F32)).astype(BF16)
        else:
            for i in range(n_q):
                r = slice(HEAD_DIM * i, HEAD_DIM * (i + 1))
                o_ref[r, cols] = (normalized(i) * sg_ref[r, cols].astype(F32)).astype(BF16)

    def body(jj, carry):
        m_old, m_run = carry
        qtile = jj // trips_per_qtile
        c0 = (jj % trips_per_qtile) * CHUNKS_PER_TRIP
        wraps = c0 + CHUNKS_PER_TRIP == n_chunks
        for u in range(CHUNKS_PER_TRIP):
            last = u == CHUNKS_PER_TRIP - 1
            q_next = jnp.minimum(qtile + wraps.astype(jnp.int32), n_qtiles - 1) if last else qtile
            c_next = jnp.where(wraps, 0, c0 + u + 1) if last else c0 + u + 1
            nxt = []
            for t, (i, lanes) in enumerate(tiles):
                nxt.append(score_tile((u + 1) % 2, q_next, c_next, i, lanes))
                value_tile(u % 2, c0 + u, i, lanes, m_old[t], m_run[t])
            if last:
                m_old = [jnp.where(wraps, -jnp.inf, a) for a in m_run]
                m_run = [jnp.where(wraps, b, jnp.maximum(a, b)) for a, b in zip(m_run, nxt)]
            else:
                m_old, m_run = m_run, [jnp.maximum(a, b) for a, b in zip(m_run, nxt)]

        @pl.when(wraps)
        def _():
            finalize(qtile)

        return m_old, m_run

    first = [score_tile(0, 0, 0, i, lanes) for i, lanes in tiles]
    minus_inf = [jnp.full((1, LANE_TILE), -jnp.inf, F32)] * len(tiles)
    lax.fori_loop(0, n_qtiles * trips_per_qtile, body, (minus_inf, first))


def _dense_attn(q_pad, k_nat, vT, sgT, sg_row0, batch, seq, diff, lam_vecs=None, subln_col=None,
                lam_init=0.0):
    n_q = 2
    out_rows = HEAD_DIM if diff else 2 * HEAD_DIM
    n_groups = BRANCH_W // out_rows
    sg_blk0 = sg_row0 // out_rows
    in_specs = [
        pl.BlockSpec((n_q, KV_PAD, seq), lambda b, g: (g, 0, b)),
        pl.BlockSpec((seq, KV_PAD), lambda b, g: (b, 0)),
        pl.BlockSpec((HEAD_DIM, seq), (lambda b, g: (g // 2, b)) if diff else (lambda b, g: (g, b))),
        pl.BlockSpec((out_rows, seq), lambda b, g: (sg_blk0 + g, b)),
    ]
    args = [q_pad, k_nat, vT, sgT]
    if diff:
        in_specs += [pl.BlockSpec((4, DIFF_DIM), lambda b, g: (0, 0)),
                     pl.BlockSpec((HEAD_DIM, 1), lambda b, g: (0, 0))]
        args += [lam_vecs, subln_col]
    return pl.pallas_call(
        functools.partial(_dense_kernel, n_q=n_q, diff=diff, lam_init=lam_init, seq=seq),
        grid=(batch, n_groups),
        in_specs=in_specs,
        out_specs=pl.BlockSpec((out_rows, seq), lambda b, g: (g, b)),
        out_shape=jax.ShapeDtypeStruct((BRANCH_W, batch * seq), BF16),
        scratch_shapes=[pltpu.VMEM((n_q, TK, TQ_DENSE), BF16), pltpu.VMEM((n_q, TK, TQ_DENSE), BF16),
                        pltpu.VMEM((n_q, V_EXT, TQ_DENSE), F32)],
        compiler_params=_params(("parallel", "parallel")),
        name="diff_attn" if diff else "dense_attn",
    )(*args)


def _softmax_value_tile(sb, v, extra_logit=None):
    m = jnp.max(sb, axis=0, keepdims=True)
    if extra_logit is not None:
        m = jnp.maximum(m.astype(F32), extra_logit).astype(BF16)
    p = jnp.exp2(sb - m)
    acc = jnp.dot(v, p, preferred_element_type=F32)
    denom = acc[HEAD_DIM:HEAD_DIM + 1]
    if extra_logit is not None:
        denom = denom + jnp.exp2(extra_logit - m.astype(F32))
    return acc[0:HEAD_DIM], denom


def _window_kernel(sink_ref, q_ref, k_ref, v_ref, sg_ref, o_ref, *, seq):
    g = pl.program_id(1)
    shape = (WIN_KEYS, LANE_TILE)
    rel = lax.broadcasted_iota(jnp.int32, shape, 0) - lax.broadcasted_iota(jnp.int32, shape, 1)

    def body(j, carry):
        pending = []
        for h in range(TQ // LANE_TILE):
            qs = pl.multiple_of(j * TQ + LANE_TILE * h, LANE_TILE)
            cols = pl.ds(qs, LANE_TILE)
            start = pl.multiple_of(jnp.clip(qs - WINDOW, 0, seq - WIN_KEYS), WINDOW)
            k = k_ref[pl.ds(start, WIN_KEYS), :]
            inside = jnp.abs(rel + (start - qs)) <= WINDOW
            for i in range(2):
                s = jnp.dot(k, q_ref[i, :, cols], preferred_element_type=F32)
                pending.append((i, cols, start, jnp.where(inside, s, NEG_INF).astype(BF16)))
        for i, cols, start, sb in pending:
            rows = slice(HEAD_DIM * i, HEAD_DIM * (i + 1))
            v = _v_ext(v_ref[:, pl.ds(start, WIN_KEYS)])
            out, denom = _softmax_value_tile(sb, v, sink_ref[2 * g + i] * LOG2E)
            o_ref[rows, cols] = (out / denom * sg_ref[rows, cols].astype(F32)).astype(BF16)
        return carry

    lax.fori_loop(0, seq // TQ, body, 0)


def _window_attn(sink, q_pad, k_nat, vT, sgT, sg_row0, batch, seq):
    sg_blk0 = sg_row0 // (2 * HEAD_DIM)
    return pl.pallas_call(
        functools.partial(_window_kernel, seq=seq),
        grid=(batch, N_HEADS // 2),
        in_specs=[
            pl.BlockSpec(memory_space=pltpu.SMEM),
            pl.BlockSpec((2, KV_PAD, seq), lambda b, g: (g, 0, b)),
            pl.BlockSpec((seq, KV_PAD), lambda b, g: (b, 0)),
            pl.BlockSpec((HEAD_DIM, seq), lambda b, g: (g, b)),
            pl.BlockSpec((2 * HEAD_DIM, seq), lambda b, g: (sg_blk0 + g, b)),
        ],
        out_specs=pl.BlockSpec((2 * HEAD_DIM, seq), lambda b, g: (g, b)),
        out_shape=jax.ShapeDtypeStruct((BRANCH_W, batch * seq), BF16),
        compiler_params=_params(("parallel", "parallel")),
        name="window_attn",
    )(sink, q_pad, k_nat, vT, sgT)


N_DR = 2 * NA_KH - 1
N_DC = 2 * NA_KW - 1


def _nbr_start_row(first_row, rows, clip):
    return clip(first_row - NA_KH // 2, 0, rows - NBR_KEY_ROWS)


def _nbr_kernel(q_ref, k_ref, v_ref, bias_ref, sg_ref, o_ref, *, seq):
    rows = seq // GRID_W

    def body(j, carry):
        pending = []
        for h in range(TQ // LANE_TILE):
            first_row = j * (TQ // GRID_W) + NBR_TILE_ROWS * h
            cols = pl.ds(pl.multiple_of(first_row * GRID_W, LANE_TILE), LANE_TILE)
            start = _nbr_start_row(first_row, rows, jnp.clip) * GRID_W
            start = pl.multiple_of(start, NBR_TILE_ROWS * GRID_W)
            variant = jnp.where(first_row == 0, 0, jnp.where(first_row == rows - NBR_TILE_ROWS, 2, 1))
            k = k_ref[pl.ds(start, NBR_KEYS), :]
            for i in range(2):
                s = jnp.dot(k, q_ref[i, :, cols], preferred_element_type=F32) + bias_ref[variant, i]
                pending.append((i, cols, start, s.astype(BF16)))
        for i, cols, start, sb in pending:
            hrows = slice(HEAD_DIM * i, HEAD_DIM * (i + 1))
            v = _v_ext(v_ref[hrows, pl.ds(start, NBR_KEYS)])
            out, denom = _softmax_value_tile(sb, v)
            o_ref[hrows, cols] = (out / denom * sg_ref[hrows, cols].astype(F32)).astype(BF16)
        return carry

    lax.fori_loop(0, seq // TQ, body, 0)


def _nbr_bias_kernel(rpb_ref, o_ref, *, seq):
    h = pl.program_id(0)
    rows = seq // GRID_W
    kh = min(NA_KH, rows)
    shape = (GRID_W, 2 * GRID_W)
    kc = lax.broadcasted_iota(jnp.int32, shape, 0)
    lane = lax.broadcasted_iota(jnp.int32, shape, 1)
    qc = lane % GRID_W
    dcol = jnp.clip(kc - qc, -(NA_KW - 1), NA_KW - 1) + NA_KW - 1
    cs = jnp.clip(qc - NA_KW // 2, 0, GRID_W - NA_KW)
    col_ok = (kc >= cs) & (kc < cs + NA_KW)
    hits = [dcol == j for j in range(N_DC)]
    toeplitz = []
    for dr in range(N_DR):
        t = jnp.zeros(shape, F32)
        for j in range(N_DC):
            t = jnp.where(hits[j], rpb_ref[(h * N_DR + dr) * N_DC + j], t)
        toeplitz.append(jnp.where(col_ok, t * LOG2E, NEG_INF))
    masked = jnp.full(shape, NEG_INF, F32)
    left = lane < GRID_W
    for vi, first_row in enumerate((0, NBR_TILE_ROWS, rows - NBR_TILE_ROWS)):
        start_row = int(_nbr_start_row(first_row, rows, np.clip))
        for kr in range(NBR_KEY_ROWS):
            krow = start_row + kr
            for qp in range(NBR_TILE_ROWS // 2):
                halves = []
                for qrow in (first_row + 2 * qp, first_row + 2 * qp + 1):
                    rs = int(np.clip(qrow - kh // 2, 0, rows - kh))
                    halves.append(toeplitz[krow - qrow + NA_KH - 1] if rs <= krow < rs + kh else masked)
                blk = halves[0] if halves[0] is halves[1] else jnp.where(left, halves[0], halves[1])
                o_ref[vi, 0, GRID_W * kr:GRID_W * (kr + 1), 2 * GRID_W * qp:2 * GRID_W * (qp + 1)] = blk


def _nbr_bias_tables(rpb, seq):
    return pl.pallas_call(
        functools.partial(_nbr_bias_kernel, seq=seq),
        grid=(N_HEADS,),
        in_specs=[pl.BlockSpec(memory_space=pltpu.SMEM)],
        out_specs=pl.BlockSpec((3, 1, NBR_KEYS, LANE_TILE), lambda h: (0, h, 0, 0)),
        out_shape=jax.ShapeDtypeStruct((3, N_HEADS, NBR_KEYS, LANE_TILE), F32),
        compiler_params=_params(("parallel",)),
        name="nbr_bias",
    )(rpb.astype(F32).reshape(-1))


def _nbr_attn(q_pad, k_nat, vT, bias, sgT, sg_row0, batch, seq):
    sg_blk0 = sg_row0 // (2 * HEAD_DIM)
    return pl.pallas_call(
        functools.partial(_nbr_kernel, seq=seq),
        grid=(batch, N_HEADS // 2),
        in_specs=[
            pl.BlockSpec((2, KV_PAD, seq), lambda b, g: (g, 0, b)),
            pl.BlockSpec((seq, KV_PAD), lambda b, g: (b, g)),
            pl.BlockSpec((2 * HEAD_DIM, seq), lambda b, g: (g, b)),
            pl.BlockSpec((3, 2, NBR_KEYS, LANE_TILE), lambda b, g: (0, g, 0, 0)),
            pl.BlockSpec((2 * HEAD_DIM, seq), lambda b, g: (sg_blk0 + g, b)),
        ],
        out_specs=pl.BlockSpec((2 * HEAD_DIM, seq), lambda b, g: (g, b)),
        out_shape=jax.ShapeDtypeStruct((BRANCH_W, batch * seq), BF16),
        compiler_params=_params(("parallel", "parallel")),
        name="nbr_attn",
    )(q_pad, k_nat, vT, bias, sgT)


def _out_proj_kernel(x_ref, ma_ref, mb_ref, mc_ref, md_ref, w_ref, *rest, token_major, final):
    acc = x_ref[...].T if token_major else x_ref[...]
    for j, m_ref in enumerate((ma_ref, mb_ref, mc_ref, md_ref)):
        acc = acc + jnp.dot(w_ref[:, BRANCH_W * j:BRANCH_W * (j + 1)], m_ref[...],
                            preferred_element_type=F32)
    if final:
        fg_ref, o_ref = rest
        ms = jnp.mean(acc * acc, axis=0, keepdims=True)
        o_ref[...] = (acc * lax.rsqrt(ms + EPS) * fg_ref[...]).T
    else:
        (o_ref,) = rest
        o_ref[...] = acc


def _out_proj(x, mixes, w_outT, token_major, final_g_col=None):
    t, d = x.shape if token_major else x.shape[::-1]
    tm = TM_OUT_PROJ
    final = final_g_col is not None
    chan = lambda w: pl.BlockSpec((w, tm), lambda i: (0, i))
    nat = pl.BlockSpec((tm, d), lambda i: (i, 0))
    in_specs = ([nat if token_major else chan(d)] + [chan(BRANCH_W)] * 4
                + [pl.BlockSpec((d, 4 * BRANCH_W), lambda i: (0, 0))])
    args = [x, *mixes, w_outT]
    if final:
        in_specs.append(pl.BlockSpec((d, 1), lambda i: (0, 0)))
        args.append(final_g_col)
        out_spec = nat
        out_shape = jax.ShapeDtypeStruct((t, d), F32)
    else:
        out_spec = chan(d)
        out_shape = jax.ShapeDtypeStruct((d, t), F32)
    return pl.pallas_call(
        functools.partial(_out_proj_kernel, token_major=token_major, final=final),
        grid=(t // tm,),
        in_specs=in_specs,
        out_specs=out_spec,
        out_shape=out_shape,
        compiler_params=_params(("parallel",)),
        name="out_proj_final" if final else "out_proj",
    )(*args)


def _rope_tables(seq):
    t = jnp.arange(seq)
    row, col = t // GRID_W, t % GRID_W

    def cs(pos, d):
        inv = jnp.power(ROPE_THETA, -jnp.arange(0, d, 2, dtype=F32) / d)
        ang = pos.astype(F32)[None, :] * inv[:, None]
        return jnp.cos(ang), jnp.sin(ang)

    cr, sr = cs(row, HEAD_DIM // 2)
    cc, sc = cs(col, HEAD_DIM // 2)
    ct, st = cs(t, HEAD_DIM)
    cdt, sdt = cs(t, DIFF_DIM)
    cat = lambda *a: jnp.concatenate(a, axis=0)
    return (cat(cr, cr, cc, cc), cat(-sr, sr, -sc, sc),
            cat(ct, ct), cat(-st, st),
            cat(cdt, cdt, cdt, cdt), cat(-sdt, sdt, -sdt, sdt))


def kernel(x, norm_g, w_in, w_out, qn_a, kn_a, sink_b, rpb_c, lam_q1, lam_k1, lam_q2, lam_k2, subln_d, final_g):
    batch, seq, d = x.shape
    depth = w_in.shape[0]
    tables = _rope_tables(seq)
    res = x.reshape(batch * seq, d)
    out = None
    for l in range(depth):
        token_major = l == 0
        w_inT = w_in[l].T.astype(BF16)
        w_outT = w_out[l].T.astype(BF16)
        (qa, ka, va, qb, kb, vb, qc, kc, vc, qd, kd, vd, sg) = _in_proj(
            res, norm_g[l].reshape(d, 1), w_inT, qn_a[l].reshape(HEAD_DIM, 1), kn_a[l].reshape(HEAD_DIM, 1),
            tables, seq, token_major)
        mix_a = _dense_attn(qa, ka, va, sg, 0, batch, seq, diff=False)
        mix_b = _window_attn(sink_b[l], qb, kb, vb, sg, BRANCH_W, batch, seq)
        mix_c = _nbr_attn(qc, kc, vc, _nbr_bias_tables(rpb_c[l], seq), sg, 2 * BRANCH_W, batch, seq)
        lam_init = 0.8 - 0.6 * math.exp(-0.3 * l)
        lam_vecs = jnp.stack([lam_q1[l], lam_k1[l], lam_q2[l], lam_k2[l]]).astype(F32)
        mix_d = _dense_attn(qd, kd, vd, sg, 3 * BRANCH_W, batch, seq, diff=True, lam_vecs=lam_vecs,
                            subln_col=subln_d[l].reshape(HEAD_DIM, 1), lam_init=lam_init)
        mixes = (mix_a, mix_b, mix_c, mix_d)
        if l == depth - 1:
            out = _out_proj(res, mixes, w_outT, token_major, final_g.reshape(d, 1))
        else:
            res = _out_proj(res, mixes, w_outT, token_major)
    return out.reshape(batch, seq, d)
```

```python
import functools
import math

import jax
import jax.numpy as jnp
import numpy as np
from jax import lax
from jax.experimental import pallas as pl
from jax.experimental.pallas import tpu as pltpu

HEAD_DIM = 64
GRID_W = 64
DIFF_DIM = 32
WINDOW = 128
NA_KH = 8
NA_KW = 16
ROPE_THETA = 10000.0
EPS = 1e-6
NEG_INF = -1e30
LOG2E = math.log2(math.e)

N_HEADS = 4
BRANCH_W = N_HEADS * HEAD_DIM
KV_PAD = 128
ONES_ROWS = 16
V_EXT = HEAD_DIM + ONES_ROWS

V7X_VMEM_BYTES = 64 * 1024 * 1024
VMEM_LIMIT = V7X_VMEM_BYTES * 3 // 4

TM_PROJ = 512
WEIGHT_PREP_STEPS = 2
TM_OUT_PROJ = 512
TQ = 1024
TQ_DENSE = 512
TK = 512
CHUNKS_PER_TRIP = 8
LANE_TILE = 256
WIN_KEYS = LANE_TILE + 2 * WINDOW
WIN_MASKS = (WIN_KEYS - LANE_TILE) // WINDOW + 1
NBR_TILE_ROWS = LANE_TILE // GRID_W
NBR_KEY_ROWS = NBR_TILE_ROWS + NA_KH
NBR_KEYS = NBR_KEY_ROWS * GRID_W

F32 = jnp.float32
BF16 = jnp.bfloat16


def _params(sem):
    return pltpu.CompilerParams(dimension_semantics=sem, vmem_limit_bytes=VMEM_LIMIT)


def _weight_prep_kernel(w_ref, o_ref):
    o_ref[0] = w_ref[0].T.astype(BF16)


def _transposed_bf16(w):
    depth, k, n = w.shape
    tn = n // WEIGHT_PREP_STEPS
    return pl.pallas_call(
        _weight_prep_kernel,
        grid=(depth, n // tn),
        in_specs=[pl.BlockSpec((1, k, tn), lambda l, j: (l, 0, j))],
        out_specs=pl.BlockSpec((1, tn, k), lambda l, j: (l, j, 0)),
        out_shape=jax.ShapeDtypeStruct((depth, n, k), BF16),
        compiler_params=_params(("parallel", "parallel")),
        name="weight_prep",
    )(w)


def _layer_weight_spec(stacked, layer):
    return pl.BlockSpec((None,) + stacked.shape[1:], lambda *_: (layer, 0, 0))


def _rot_quarters(y):
    return jnp.concatenate([y[16:32], y[0:16], y[48:64], y[32:48]], axis=0)


def _rot_halves(y):
    return jnp.concatenate([y[32:64], y[0:32]], axis=0)


def _head_rms(y, g_col):
    ms = jnp.mean(y * y, axis=0, keepdims=True)
    return y * lax.rsqrt(ms + EPS) * g_col


def _pad_rows(y, slot, n_slots):
    z = jnp.zeros_like(y)
    return jnp.concatenate([y if s == slot else z for s in range(n_slots)], axis=0)


def _silu(g):
    return g * (1.0 / (1.0 + jnp.exp(-g)))


def _residual_update(x, mix_refs, wo_ref):
    for j, m_ref in enumerate(mix_refs):
        x = x + jnp.dot(wo_ref[:, BRANCH_W * j:BRANCH_W * (j + 1)], m_ref[...], preferred_element_type=F32)
    return x


def _in_proj_kernel(*refs, token_major, after_out_proj):
    x_ref, refs = refs[0], refs[1:]
    if after_out_proj:
        mix_refs, wo_ref, refs = refs[0:4], refs[4], refs[5:]
    (ng_ref, w_ref, qn_ref, kn_ref, ca_ref, sa_ref, cb_ref, sb_ref, cd_ref, sd_ref), refs = refs[:10], refs[10:]
    if after_out_proj:
        res_ref, refs = refs[0], refs[1:]
    (qa_ref, ka_ref, va_ref, qb_ref, kb_ref, vb_ref,
     qc_ref, kc_ref, vc_ref, qd_ref, kd_ref, vd_ref, sg_ref, h_ref) = refs

    x = x_ref[...].T if token_major else x_ref[...]
    if after_out_proj:
        x = _residual_update(x, mix_refs, wo_ref)
        res_ref[...] = x
    ms = jnp.mean(x * x, axis=0, keepdims=True)
    h_ref[...] = (x * lax.rsqrt(ms + EPS) * ng_ref[...]).astype(BF16)

    def proj(r0, nrows):
        return jnp.dot(w_ref[r0:r0 + nrows, :], h_ref[...], preferred_element_type=F32)

    sc_hd = HEAD_DIM ** -0.5
    ca, sa = ca_ref[...], sa_ref[...]
    cb, sb = cb_ref[...], sb_ref[...]
    cd, sd = cd_ref[...], sd_ref[...]
    qn, kn = qn_ref[...], kn_ref[...]

    def rope_a(y):
        return y * ca + _rot_quarters(y) * sa

    def rope_b(y):
        return y * cb + _rot_halves(y) * sb

    def rope_d(y):
        return y * cd + _rot_quarters(y) * sd

    base = 0
    z = proj(base, 256)
    for hh in range(N_HEADS):
        y = rope_a(_head_rms(z[64 * hh:64 * hh + 64], qn)) * (sc_hd * LOG2E)
        qa_ref[hh] = _pad_rows(y, hh // 2, 2).astype(BF16)
    z = proj(base + 256, 128)
    kk = jnp.concatenate([rope_a(_head_rms(z[64 * j:64 * j + 64], kn)) for j in range(2)], axis=0)
    ka_ref[...] = kk.T.astype(BF16)
    va_ref[...] = proj(base + 384, 128).astype(BF16)
    sg_ref[0:256, :] = _silu(proj(base + 512, 256)).astype(BF16)

    base = 768
    z = proj(base, 256)
    for hh in range(N_HEADS):
        y = rope_b(z[64 * hh:64 * hh + 64]) * (sc_hd * LOG2E)
        qb_ref[hh] = _pad_rows(y, hh // 2, 2).astype(BF16)
    z = proj(base + 256, 128)
    kk = jnp.concatenate([rope_b(z[64 * j:64 * j + 64]) for j in range(2)], axis=0)
    kb_ref[...] = kk.T.astype(BF16)
    vb_ref[...] = proj(base + 384, 128).astype(BF16)
    sg_ref[256:512, :] = _silu(proj(base + 512, 256)).astype(BF16)

    base = 1536
    z = proj(base, 256)
    for hh in range(N_HEADS):
        y = z[64 * hh:64 * hh + 64] * (sc_hd * LOG2E)
        qc_ref[hh] = _pad_rows(y, hh % 2, 2).astype(BF16)
    z = proj(base + 256, 256)
    kc_ref[...] = z.T.astype(BF16)
    vc_ref[...] = proj(base + 512, 256).astype(BF16)
    sg_ref[512:768, :] = _silu(proj(base + 768, 256)).astype(BF16)

    base = 2560
    z = proj(base, 256)
    sc_dd = DIFF_DIM ** -0.5
    for hh in range(N_HEADS):
        y = rope_d(z[64 * hh:64 * hh + 64]) * (sc_dd * LOG2E)
        for c in range(2):
            qd_ref[2 * hh + c] = _pad_rows(y[32 * c:32 * c + 32], 2 * (hh // 2) + c, 4).astype(BF16)
    z = proj(base + 256, 128)
    kk = jnp.concatenate([rope_d(z[64 * j:64 * j + 64]) for j in range(2)], axis=0)
    kd_ref[...] = kk.T.astype(BF16)
    sg_ref[768:1024, :] = _silu(proj(base + 512, 256)).astype(BF16)
    vd_ref[...] = proj(base + 384, 128).astype(BF16)


def _in_proj(x, ng_col, w_inT, layer, qn_col, kn_col, tables, seq, token_major, prev=None):
    t, d = x.shape if token_major else x.shape[::-1]
    tm = TM_PROJ
    nt_seq = seq // tm
    full = lambda shape: pl.BlockSpec(shape, lambda i: (0,) * len(shape))
    tab = pl.BlockSpec((HEAD_DIM, tm), lambda i: (0, i % nt_seq))
    qpad = lambda n: pl.BlockSpec((n, KV_PAD, tm), lambda i: (0, 0, i))
    nat = lambda w: pl.BlockSpec((tm, w), lambda i: (i, 0))
    chan = lambda w: pl.BlockSpec((w, tm), lambda i: (0, i))
    sds = jax.ShapeDtypeStruct
    out_shape = [
        sds((N_HEADS, KV_PAD, t), BF16), sds((t, 128), BF16), sds((128, t), BF16),
        sds((N_HEADS, KV_PAD, t), BF16), sds((t, 128), BF16), sds((128, t), BF16),
        sds((N_HEADS, KV_PAD, t), BF16), sds((t, 256), BF16), sds((256, t), BF16),
        sds((2 * N_HEADS, KV_PAD, t), BF16), sds((t, 128), BF16), sds((128, t), BF16),
        sds((4 * BRANCH_W, t), BF16),
    ]
    out_specs = [
        qpad(N_HEADS), nat(128), chan(128),
        qpad(N_HEADS), nat(128), chan(128),
        qpad(N_HEADS), nat(256), chan(256),
        qpad(2 * N_HEADS), nat(128), chan(128),
        chan(4 * BRANCH_W),
    ]
    in_specs = [nat(d) if token_major else chan(d)]
    args = [x]
    if prev is not None:
        mixes, w_outT = prev
        in_specs += [chan(BRANCH_W)] * len(mixes) + [_layer_weight_spec(w_outT, layer - 1)]
        args += [*mixes, w_outT]
        out_specs = [chan(d)] + out_specs
        out_shape = [sds((d, t), F32)] + out_shape
    in_specs += [full((d, 1)), _layer_weight_spec(w_inT, layer), full((HEAD_DIM, 1)), full((HEAD_DIM, 1)),
                 tab, tab, tab, tab, tab, tab]
    args += [ng_col, w_inT, qn_col, kn_col, *tables]
    return pl.pallas_call(
        functools.partial(_in_proj_kernel, token_major=token_major, after_out_proj=prev is not None),
        grid=(t // tm,),
        in_specs=in_specs,
        out_specs=out_specs,
        out_shape=out_shape,
        scratch_shapes=[pltpu.VMEM((d, tm), BF16)],
        compiler_params=_params(("parallel",)),
        name="in_proj",
    )(*args)


def _v_ext(v):
    return jnp.concatenate([v, jnp.ones((ONES_ROWS, v.shape[1]), v.dtype)], axis=0)


def _dense_kernel(*refs, n_q, diff, lam_init, seq):
    if diff:
        (q_ref, k_ref, v_ref, sg_ref, lam_ref, sub_ref, o_ref, s0_ref, s1_ref, acc_ref) = refs
    else:
        (q_ref, k_ref, v_ref, sg_ref, o_ref, s0_ref, s1_ref, acc_ref) = refs
    s_slots = (s0_ref, s1_ref)
    n_chunks = seq // TK
    n_qtiles = seq // TQ_DENSE
    trips_per_qtile = n_chunks // CHUNKS_PER_TRIP

    acc_ref[...] = jnp.zeros(acc_ref.shape, F32)

    tiles = [(i, slice(LANE_TILE * h, LANE_TILE * (h + 1))) for i in range(n_q) for h in range(TQ_DENSE // LANE_TILE)]

    def score_tile(slot, qtile, chunk, i, lanes):
        k = k_ref[pl.ds(pl.multiple_of(chunk * TK, TK), TK), :]
        q = q_ref[i, :, pl.ds(pl.multiple_of(qtile * TQ_DENSE + lanes.start, LANE_TILE), LANE_TILE)]
        s = jnp.dot(k, q, preferred_element_type=F32).astype(BF16)
        s_slots[slot][i, :, lanes] = s
        return jnp.max(s, axis=0, keepdims=True).astype(F32)

    def value_tile(slot, chunk, i, lanes, m_old, m_new):
        v = _v_ext(v_ref[:, pl.ds(pl.multiple_of(chunk * TK, TK), TK)])
        alpha = jnp.exp2(m_old - m_new)
        p = jnp.exp2(s_slots[slot][i, :, lanes] - m_new.astype(BF16))
        acc_ref[i, :, lanes] = alpha * acc_ref[i, :, lanes] + jnp.dot(v, p, preferred_element_type=F32)

    def normalized(i):
        acc = acc_ref[i]
        return acc[0:HEAD_DIM] / acc[HEAD_DIM:HEAD_DIM + 1]

    def finalize(qtile):
        cols = pl.ds(pl.multiple_of(qtile * TQ_DENSE, TQ_DENSE), TQ_DENSE)
        if diff:
            lv = lam_ref[...]
            lam = (jnp.exp(jnp.sum(lv[0:1] * lv[1:2], axis=1, keepdims=True))
                   - jnp.exp(jnp.sum(lv[2:3] * lv[3:4], axis=1, keepdims=True)) + lam_init)
            o = normalized(0) - lam * normalized(1)
            ms = jnp.mean(o * o, axis=0, keepdims=True)
            o = o * lax.rsqrt(ms + EPS) * sub_ref[...] * (1.0 - lam_init)
            o_ref[:, cols] = (o * sg_ref[:, cols].astype(F32)).astype(BF16)
        else:
            for i in range(n_q):
                r = slice(HEAD_DIM * i, HEAD_DIM * (i + 1))
                o_ref[r, cols] = (normalized(i) * sg_ref[r, cols].astype(F32)).astype(BF16)

    def body(jj, carry):
        m_old, m_run = carry
        qtile = jj // trips_per_qtile
        c0 = (jj % trips_per_qtile) * CHUNKS_PER_TRIP
        wraps = c0 + CHUNKS_PER_TRIP == n_chunks
        for u in range(CHUNKS_PER_TRIP):
            last = u == CHUNKS_PER_TRIP - 1
            q_next = jnp.minimum(jnp.where(wraps, qtile + 1, qtile), n_qtiles - 1) if last else qtile
            c_next = jnp.where(wraps, 0, c0 + u + 1) if last else c0 + u + 1
            nxt = []
            for t, (i, lanes) in enumerate(tiles):
                nxt.append(score_tile((u + 1) % 2, q_next, c_next, i, lanes))
                value_tile(u % 2, c0 + u, i, lanes, m_old[t], m_run[t])
            if last:
                m_old = [jnp.where(wraps, -jnp.inf, a) for a in m_run]
                m_run = [jnp.where(wraps, b, jnp.maximum(a, b)) for a, b in zip(m_run, nxt)]
            else:
                m_old, m_run = m_run, [jnp.maximum(a, b) for a, b in zip(m_run, nxt)]

        @pl.when(wraps)
        def _():
            finalize(qtile)

        return m_old, m_run

    first = [score_tile(0, 0, 0, i, lanes) for i, lanes in tiles]
    minus_inf = [jnp.full((1, LANE_TILE), -jnp.inf, F32)] * len(tiles)
    lax.fori_loop(0, n_qtiles * trips_per_qtile, body, (minus_inf, first))


def _dense_attn(q_pad, k_nat, vT, sgT, sg_row0, batch, seq, diff, lam_vecs=None, subln_col=None,
                lam_init=0.0):
    n_q = 2
    out_rows = HEAD_DIM if diff else 2 * HEAD_DIM
    n_groups = BRANCH_W // out_rows
    sg_blk0 = sg_row0 // out_rows
    in_specs = [
        pl.BlockSpec((n_q, KV_PAD, seq), lambda b, g: (g, 0, b)),
        pl.BlockSpec((seq, KV_PAD), lambda b, g: (b, 0)),
        pl.BlockSpec((HEAD_DIM, seq), (lambda b, g: (g // 2, b)) if diff else (lambda b, g: (g, b))),
        pl.BlockSpec((out_rows, seq), lambda b, g: (sg_blk0 + g, b)),
    ]
    args = [q_pad, k_nat, vT, sgT]
    if diff:
        in_specs += [pl.BlockSpec((4, DIFF_DIM), lambda b, g: (0, 0)),
                     pl.BlockSpec((HEAD_DIM, 1), lambda b, g: (0, 0))]
        args += [lam_vecs, subln_col]
    return pl.pallas_call(
        functools.partial(_dense_kernel, n_q=n_q, diff=diff, lam_init=lam_init, seq=seq),
        grid=(batch, n_groups),
        in_specs=in_specs,
        out_specs=pl.BlockSpec((out_rows, seq), lambda b, g: (g, b)),
        out_shape=jax.ShapeDtypeStruct((BRANCH_W, batch * seq), BF16),
        scratch_shapes=[pltpu.VMEM((n_q, TK, TQ_DENSE), BF16), pltpu.VMEM((n_q, TK, TQ_DENSE), BF16),
                        pltpu.VMEM((n_q, V_EXT, TQ_DENSE), F32)],
        compiler_params=_params(("parallel", "parallel")),
        name="diff_attn" if diff else "dense_attn",
    )(*args)


def _softmax_value_tile(sb, v, extra_logit=None):
    m = jnp.max(sb, axis=0, keepdims=True)
    if extra_logit is not None:
        m = jnp.maximum(m.astype(F32), extra_logit).astype(BF16)
    p = jnp.exp2(sb - m)
    acc = jnp.dot(v, p, preferred_element_type=F32)
    denom = acc[HEAD_DIM:HEAD_DIM + 1]
    if extra_logit is not None:
        denom = denom + jnp.exp2(extra_logit - m.astype(F32))
    return acc[0:HEAD_DIM], denom


def _window_kernel(sink_ref, q_ref, k_ref, v_ref, sg_ref, o_ref, mask_ref, *, seq):
    g = pl.program_id(1)
    shape = (WIN_KEYS, LANE_TILE)
    rel = lax.broadcasted_iota(jnp.int32, shape, 0) - lax.broadcasted_iota(jnp.int32, shape, 1)
    for variant in range(WIN_MASKS):
        mask_ref[variant] = jnp.where(jnp.abs(rel - variant * WINDOW) <= WINDOW, 0.0, NEG_INF).astype(BF16)

    def body(j, carry):
        pending = []
        for h in range(TQ // LANE_TILE):
            qs = pl.multiple_of(j * TQ + LANE_TILE * h, LANE_TILE)
            cols = pl.ds(qs, LANE_TILE)
            start = pl.multiple_of(jnp.clip(qs - WINDOW, 0, seq - WIN_KEYS), WINDOW)
            k = k_ref[pl.ds(start, WIN_KEYS), :]
            mask = mask_ref[(qs - start) // WINDOW]
            for i in range(2):
                s = jnp.dot(k, q_ref[i, :, cols], preferred_element_type=F32)
                pending.append((i, cols, start, s.astype(BF16) + mask))
        for i, cols, start, sb in pending:
            rows = slice(HEAD_DIM * i, HEAD_DIM * (i + 1))
            v = _v_ext(v_ref[:, pl.ds(start, WIN_KEYS)])
            out, denom = _softmax_value_tile(sb, v, sink_ref[2 * g + i] * LOG2E)
            o_ref[rows, cols] = (out / denom * sg_ref[rows, cols].astype(F32)).astype(BF16)
        return carry

    lax.fori_loop(0, seq // TQ, body, 0)


def _window_attn(sink, q_pad, k_nat, vT, sgT, sg_row0, batch, seq):
    sg_blk0 = sg_row0 // (2 * HEAD_DIM)
    return pl.pallas_call(
        functools.partial(_window_kernel, seq=seq),
        grid=(batch, N_HEADS // 2),
        in_specs=[
            pl.BlockSpec(memory_space=pltpu.SMEM),
            pl.BlockSpec((2, KV_PAD, seq), lambda b, g: (g, 0, b)),
            pl.BlockSpec((seq, KV_PAD), lambda b, g: (b, 0)),
            pl.BlockSpec((HEAD_DIM, seq), lambda b, g: (g, b)),
            pl.BlockSpec((2 * HEAD_DIM, seq), lambda b, g: (sg_blk0 + g, b)),
        ],
        out_specs=pl.BlockSpec((2 * HEAD_DIM, seq), lambda b, g: (g, b)),
        out_shape=jax.ShapeDtypeStruct((BRANCH_W, batch * seq), BF16),
        scratch_shapes=[pltpu.VMEM((WIN_MASKS, WIN_KEYS, LANE_TILE), BF16)],
        compiler_params=_params(("parallel", "parallel")),
        name="window_attn",
    )(sink, q_pad, k_nat, vT, sgT)


N_DR = 2 * NA_KH - 1
N_DC = 2 * NA_KW - 1


def _nbr_start_row(first_row, rows, clip):
    return clip(first_row - NA_KH // 2, 0, rows - NBR_KEY_ROWS)


def _nbr_kernel(q_ref, k_ref, v_ref, bias_ref, sg_ref, o_ref, *, seq):
    rows = seq // GRID_W

    def body(j, carry):
        pending = []
        for h in range(TQ // LANE_TILE):
            first_row = j * (TQ // GRID_W) + NBR_TILE_ROWS * h
            cols = pl.ds(pl.multiple_of(first_row * GRID_W, LANE_TILE), LANE_TILE)
            start = _nbr_start_row(first_row, rows, jnp.clip) * GRID_W
            start = pl.multiple_of(start, NBR_TILE_ROWS * GRID_W)
            variant = jnp.where(first_row == 0, 0, jnp.where(first_row == rows - NBR_TILE_ROWS, 2, 1))
            k = k_ref[pl.ds(start, NBR_KEYS), :]
            for i in range(2):
                s = jnp.dot(k, q_ref[i, :, cols], preferred_element_type=F32) + bias_ref[variant, i]
                pending.append((i, cols, start, s.astype(BF16)))
        for i, cols, start, sb in pending:
            hrows = slice(HEAD_DIM * i, HEAD_DIM * (i + 1))
            v = _v_ext(v_ref[hrows, pl.ds(start, NBR_KEYS)])
            out, denom = _softmax_value_tile(sb, v)
            o_ref[hrows, cols] = (out / denom * sg_ref[hrows, cols].astype(F32)).astype(BF16)
        return carry

    lax.fori_loop(0, seq // TQ, body, 0)


def _nbr_bias_kernel(rpb_ref, o_ref, *, seq):
    h = pl.program_id(0)
    rows = seq // GRID_W
    kh = min(NA_KH, rows)
    shape = (GRID_W, 2 * GRID_W)
    kc = lax.broadcasted_iota(jnp.int32, shape, 0)
    lane = lax.broadcasted_iota(jnp.int32, shape, 1)
    qc = lane % GRID_W
    dcol = jnp.clip(kc - qc, -(NA_KW - 1), NA_KW - 1) + NA_KW - 1
    cs = jnp.clip(qc - NA_KW // 2, 0, GRID_W - NA_KW)
    col_ok = (kc >= cs) & (kc < cs + NA_KW)
    hits = [dcol == j for j in range(N_DC)]
    toeplitz = []
    for dr in range(N_DR):
        t = jnp.zeros(shape, F32)
        for j in range(N_DC):
            t = jnp.where(hits[j], rpb_ref[(h * N_DR + dr) * N_DC + j], t)
        toeplitz.append(jnp.where(col_ok, t * LOG2E, NEG_INF))
    masked = jnp.full(shape, NEG_INF, F32)
    left = lane < GRID_W
    for vi, first_row in enumerate((0, NBR_TILE_ROWS, rows - NBR_TILE_ROWS)):
        start_row = int(_nbr_start_row(first_row, rows, np.clip))
        for kr in range(NBR_KEY_ROWS):
            krow = start_row + kr
            for qp in range(NBR_TILE_ROWS // 2):
                halves = []
                for qrow in (first_row + 2 * qp, first_row + 2 * qp + 1):
                    rs = int(np.clip(qrow - kh // 2, 0, rows - kh))
                    halves.append(toeplitz[krow - qrow + NA_KH - 1] if rs <= krow < rs + kh else masked)
                blk = halves[0] if halves[0] is halves[1] else jnp.where(left, halves[0], halves[1])
                o_ref[vi, 0, GRID_W * kr:GRID_W * (kr + 1), 2 * GRID_W * qp:2 * GRID_W * (qp + 1)] = blk


def _nbr_bias_tables(rpb, seq):
    return pl.pallas_call(
        functools.partial(_nbr_bias_kernel, seq=seq),
        grid=(N_HEADS,),
        in_specs=[pl.BlockSpec(memory_space=pltpu.SMEM)],
        out_specs=pl.BlockSpec((3, 1, NBR_KEYS, LANE_TILE), lambda h: (0, h, 0, 0)),
        out_shape=jax.ShapeDtypeStruct((3, N_HEADS, NBR_KEYS, LANE_TILE), F32),
        compiler_params=_params(("parallel",)),
        name="nbr_bias",
    )(rpb.astype(F32).reshape(-1))


def _nbr_attn(q_pad, k_nat, vT, bias, sgT, sg_row0, batch, seq):
    sg_blk0 = sg_row0 // (2 * HEAD_DIM)
    return pl.pallas_call(
        functools.partial(_nbr_kernel, seq=seq),
        grid=(batch, N_HEADS // 2),
        in_specs=[
            pl.BlockSpec((2, KV_PAD, seq), lambda b, g: (g, 0, b)),
            pl.BlockSpec((seq, KV_PAD), lambda b, g: (b, g)),
            pl.BlockSpec((2 * HEAD_DIM, seq), lambda b, g: (g, b)),
            pl.BlockSpec((3, 2, NBR_KEYS, LANE_TILE), lambda b, g: (0, g, 0, 0)),
            pl.BlockSpec((2 * HEAD_DIM, seq), lambda b, g: (sg_blk0 + g, b)),
        ],
        out_specs=pl.BlockSpec((2 * HEAD_DIM, seq), lambda b, g: (g, b)),
        out_shape=jax.ShapeDtypeStruct((BRANCH_W, batch * seq), BF16),
        compiler_params=_params(("parallel", "parallel")),
        name="nbr_attn",
    )(q_pad, k_nat, vT, bias, sgT)


def _out_proj_kernel(x_ref, ma_ref, mb_ref, mc_ref, md_ref, w_ref, fg_ref, o_ref, *, token_major):
    acc = x_ref[...].T if token_major else x_ref[...]
    acc = _residual_update(acc, (ma_ref, mb_ref, mc_ref, md_ref), w_ref)
    ms = jnp.mean(acc * acc, axis=0, keepdims=True)
    o_ref[...] = (acc * lax.rsqrt(ms + EPS) * fg_ref[...]).T


def _out_proj_final(x, mixes, w_outT, layer, token_major, final_g_col):
    t, d = x.shape if token_major else x.shape[::-1]
    tm = TM_OUT_PROJ
    chan = lambda w: pl.BlockSpec((w, tm), lambda i: (0, i))
    nat = pl.BlockSpec((tm, d), lambda i: (i, 0))
    return pl.pallas_call(
        functools.partial(_out_proj_kernel, token_major=token_major),
        grid=(t // tm,),
        in_specs=([nat if token_major else chan(d)] + [chan(BRANCH_W)] * 4
                  + [_layer_weight_spec(w_outT, layer), pl.BlockSpec((d, 1), lambda i: (0, 0))]),
        out_specs=nat,
        out_shape=jax.ShapeDtypeStruct((t, d), F32),
        compiler_params=_params(("parallel",)),
        name="out_proj_final",
    )(x, *mixes, w_outT, final_g_col)


def _rope_tables(seq):
    t = jnp.arange(seq)
    row, col = t // GRID_W, t % GRID_W

    def cs(pos, d):
        inv = jnp.power(ROPE_THETA, -jnp.arange(0, d, 2, dtype=F32) / d)
        ang = pos.astype(F32)[None, :] * inv[:, None]
        return jnp.cos(ang), jnp.sin(ang)

    cr, sr = cs(row, HEAD_DIM // 2)
    cc, sc = cs(col, HEAD_DIM // 2)
    ct, st = cs(t, HEAD_DIM)
    cdt, sdt = cs(t, DIFF_DIM)
    cat = lambda *a: jnp.concatenate(a, axis=0)
    return (cat(cr, cr, cc, cc), cat(-sr, sr, -sc, sc),
            cat(ct, ct), cat(-st, st),
            cat(cdt, cdt, cdt, cdt), cat(-sdt, sdt, -sdt, sdt))


def kernel(x, norm_g, w_in, w_out, qn_a, kn_a, sink_b, rpb_c, lam_q1, lam_k1, lam_q2, lam_k2, subln_d, final_g):
    batch, seq, d = x.shape
    depth = w_in.shape[0]
    tables = _rope_tables(seq)
    res = x.reshape(batch * seq, d)
    token_major = True
    pending = None
    w_inT = _transposed_bf16(w_in)
    w_outT = _transposed_bf16(w_out)
    for l in range(depth):
        outs = _in_proj(res, norm_g[l].reshape(d, 1), w_inT, l, qn_a[l].reshape(HEAD_DIM, 1),
                        kn_a[l].reshape(HEAD_DIM, 1), tables, seq, token_major, pending)
        if pending is not None:
            res, token_major = outs[0], False
            outs = outs[1:]
        (qa, ka, va, qb, kb, vb, qc, kc, vc, qd, kd, vd, sg) = outs
        mix_a = _dense_attn(qa, ka, va, sg, 0, batch, seq, diff=False)
        mix_b = _window_attn(sink_b[l], qb, kb, vb, sg, BRANCH_W, batch, seq)
        mix_c = _nbr_attn(qc, kc, vc, _nbr_bias_tables(rpb_c[l], seq), sg, 2 * BRANCH_W, batch, seq)
        lam_init = 0.8 - 0.6 * math.exp(-0.3 * l)
        lam_vecs = jnp.stack([lam_q1[l], lam_k1[l], lam_q2[l], lam_k2[l]]).astype(F32)
        mix_d = _dense_attn(qd, kd, vd, sg, 3 * BRANCH_W, batch, seq, diff=True, lam_vecs=lam_vecs,
                            subln_col=subln_d[l].reshape(HEAD_DIM, 1), lam_init=lam_init)
        pending = ((mix_a, mix_b, mix_c, mix_d), w_outT)
    out = _out_proj_final(res, *pending, depth - 1, token_major, final_g.reshape(d, 1))
    return out.reshape(batch, seq, d)
```

```python
import functools
import math

import jax
import jax.numpy as jnp
import numpy as np
from jax import lax
from jax.experimental import pallas as pl
from jax.experimental.pallas import tpu as pltpu

HEAD_DIM = 64
GRID_W = 64
DIFF_DIM = 32
WINDOW = 128
NA_KH = 8
NA_KW = 16
ROPE_THETA = 10000.0
EPS = 1e-6
NEG_INF = -1e30
LOG2E = math.log2(math.e)

N_HEADS = 4
BRANCH_W = N_HEADS * HEAD_DIM
KV_PAD = 128
ONES_ROWS = 16
V_EXT = HEAD_DIM + ONES_ROWS

V7X_VMEM_BYTES = 64 * 1024 * 1024
VMEM_LIMIT = V7X_VMEM_BYTES * 3 // 4

TM_PROJ = 512
WEIGHT_PREP_STEPS = 2
TM_OUT_PROJ = 512
TQ = 1024
TQ_DENSE = 512
TK = 512
CHUNKS_PER_TRIP = 8
LANE_TILE = 256
WIN_KEYS = LANE_TILE + 2 * WINDOW
WIN_MASKS = (WIN_KEYS - LANE_TILE) // WINDOW + 1
NBR_TILE_ROWS = LANE_TILE // GRID_W
NBR_KEY_ROWS = NBR_TILE_ROWS + NA_KH
NBR_KEYS = NBR_KEY_ROWS * GRID_W

F32 = jnp.float32
BF16 = jnp.bfloat16


def _params(sem):
    return pltpu.CompilerParams(dimension_semantics=sem, vmem_limit_bytes=VMEM_LIMIT)


def _weight_prep_kernel(w_ref, o_ref):
    o_ref[0] = w_ref[0].T.astype(BF16)


def _transposed_bf16(w):
    depth, k, n = w.shape
    tn = n // WEIGHT_PREP_STEPS
    return pl.pallas_call(
        _weight_prep_kernel,
        grid=(depth, n // tn),
        in_specs=[pl.BlockSpec((1, k, tn), lambda l, j: (l, 0, j))],
        out_specs=pl.BlockSpec((1, tn, k), lambda l, j: (l, j, 0)),
        out_shape=jax.ShapeDtypeStruct((depth, n, k), BF16),
        compiler_params=_params(("parallel", "parallel")),
        name="weight_prep",
    )(w)


def _layer_weight_spec(stacked, layer):
    return pl.BlockSpec((None,) + stacked.shape[1:], lambda *_: (layer, 0, 0))


def _rot_quarters(y):
    return jnp.concatenate([y[16:32], y[0:16], y[48:64], y[32:48]], axis=0)


def _rot_halves(y):
    return jnp.concatenate([y[32:64], y[0:32]], axis=0)


def _head_rms(y, g_col):
    ms = jnp.mean(y * y, axis=0, keepdims=True)
    return y * lax.rsqrt(ms + EPS) * g_col


def _pad_rows(y, slot, n_slots):
    z = jnp.zeros_like(y)
    return jnp.concatenate([y if s == slot else z for s in range(n_slots)], axis=0)


def _silu(g):
    return g * (1.0 / (1.0 + jnp.exp(-g)))


def _residual_update(x, mix_refs, wo_ref):
    for j, m_ref in enumerate(mix_refs):
        x = x + jnp.dot(wo_ref[:, BRANCH_W * j:BRANCH_W * (j + 1)], m_ref[...], preferred_element_type=F32)
    return x


def _in_proj_kernel(*refs, token_major, after_out_proj):
    x_ref, refs = refs[0], refs[1:]
    if after_out_proj:
        mix_refs, wo_ref, refs = refs[0:4], refs[4], refs[5:]
    (ng_ref, w_ref, qn_ref, kn_ref, ca_ref, sa_ref, cb_ref, sb_ref, cd_ref, sd_ref), refs = refs[:10], refs[10:]
    if after_out_proj:
        res_ref, refs = refs[0], refs[1:]
    (qa_ref, ka_ref, va_ref, qb_ref, kb_ref, vb_ref,
     qc_ref, kc_ref, vc_ref, qd_ref, kd_ref, vd_ref, sg_ref, h_ref) = refs

    x = x_ref[...].T if token_major else x_ref[...]
    if after_out_proj:
        x = _residual_update(x, mix_refs, wo_ref)
        res_ref[...] = x
    ms = jnp.mean(x * x, axis=0, keepdims=True)
    h_ref[...] = (x * lax.rsqrt(ms + EPS) * ng_ref[...]).astype(BF16)

    def proj(r0, nrows):
        return jnp.dot(w_ref[r0:r0 + nrows, :], h_ref[...], preferred_element_type=F32)

    sc_hd = HEAD_DIM ** -0.5
    ca, sa = ca_ref[...], sa_ref[...]
    cb, sb = cb_ref[...], sb_ref[...]
    cd, sd = cd_ref[...], sd_ref[...]
    qn, kn = qn_ref[...], kn_ref[...]

    def rope_a(y):
        return y * ca + _rot_quarters(y) * sa

    def rope_b(y):
        return y * cb + _rot_halves(y) * sb

    def rope_d(y):
        return y * cd + _rot_quarters(y) * sd

    base = 0
    z = proj(base, 256)
    for hh in range(N_HEADS):
        y = rope_a(_head_rms(z[64 * hh:64 * hh + 64], qn)) * (sc_hd * LOG2E)
        qa_ref[hh] = _pad_rows(y, hh // 2, 2).astype(BF16)
    z = proj(base + 256, 128)
    kk = jnp.concatenate([rope_a(_head_rms(z[64 * j:64 * j + 64], kn)) for j in range(2)], axis=0)
    ka_ref[...] = kk.T.astype(BF16)
    va_ref[...] = proj(base + 384, 128).astype(BF16)
    sg_ref[0:256, :] = _silu(proj(base + 512, 256)).astype(BF16)

    base = 768
    z = proj(base, 256)
    for hh in range(N_HEADS):
        y = rope_b(z[64 * hh:64 * hh + 64]) * (sc_hd * LOG2E)
        qb_ref[hh] = _pad_rows(y, hh // 2, 2).astype(BF16)
    z = proj(base + 256, 128)
    kk = jnp.concatenate([rope_b(z[64 * j:64 * j + 64]) for j in range(2)], axis=0)
    kb_ref[...] = kk.T.astype(BF16)
    vb_ref[...] = proj(base + 384, 128).astype(BF16)
    sg_ref[256:512, :] = _silu(proj(base + 512, 256)).astype(BF16)

    base = 1536
    z = proj(base, 256)
    for hh in range(N_HEADS):
        y = z[64 * hh:64 * hh + 64] * (sc_hd * LOG2E)
        qc_ref[hh] = _pad_rows(y, hh % 2, 2).astype(BF16)
    z = proj(base + 256, 256)
    kc_ref[...] = z.T.astype(BF16)
    vc_ref[...] = proj(base + 512, 256).astype(BF16)
    sg_ref[512:768, :] = _silu(proj(base + 768, 256)).astype(BF16)

    base = 2560
    z = proj(base, 256)
    sc_dd = DIFF_DIM ** -0.5
    for hh in range(N_HEADS):
        y = rope_d(z[64 * hh:64 * hh + 64]) * (sc_dd * LOG2E)
        for c in range(2):
            qd_ref[2 * hh + c] = _pad_rows(y[32 * c:32 * c + 32], 2 * (hh // 2) + c, 4).astype(BF16)
    z = proj(base + 256, 128)
    kk = jnp.concatenate([rope_d(z[64 * j:64 * j + 64]) for j in range(2)], axis=0)
    kd_ref[...] = kk.T.astype(BF16)
    sg_ref[768:1024, :] = _silu(proj(base + 512, 256)).astype(BF16)
    vd_ref[...] = proj(base + 384, 128).astype(BF16)


def _in_proj(x, ng_col, w_inT, layer, qn_col, kn_col, tables, seq, token_major, prev=None):
    t, d = x.shape if token_major else x.shape[::-1]
    tm = TM_PROJ
    nt_seq = seq // tm
    full = lambda shape: pl.BlockSpec(shape, lambda i: (0,) * len(shape))
    tab = pl.BlockSpec((HEAD_DIM, tm), lambda i: (0, i % nt_seq))
    qpad = lambda n: pl.BlockSpec((n, KV_PAD, tm), lambda i: (0, 0, i))
    nat = lambda w: pl.BlockSpec((tm, w), lambda i: (i, 0))
    chan = lambda w: pl.BlockSpec((w, tm), lambda i: (0, i))
    sds = jax.ShapeDtypeStruct
    out_shape = [
        sds((N_HEADS, KV_PAD, t), BF16), sds((t, 128), BF16), sds((128, t), BF16),
        sds((N_HEADS, KV_PAD, t), BF16), sds((t, 128), BF16), sds((128, t), BF16),
        sds((N_HEADS, KV_PAD, t), BF16), sds((t, 256), BF16), sds((256, t), BF16),
        sds((2 * N_HEADS, KV_PAD, t), BF16), sds((t, 128), BF16), sds((128, t), BF16),
        sds((4 * BRANCH_W, t), BF16),
    ]
    out_specs = [
        qpad(N_HEADS), nat(128), chan(128),
        qpad(N_HEADS), nat(128), chan(128),
        qpad(N_HEADS), nat(256), chan(256),
        qpad(2 * N_HEADS), nat(128), chan(128),
        chan(4 * BRANCH_W),
    ]
    in_specs = [nat(d) if token_major else chan(d)]
    args = [x]
    if prev is not None:
        mixes, w_outT = prev
        in_specs += [chan(BRANCH_W)] * len(mixes) + [_layer_weight_spec(w_outT, layer - 1)]
        args += [*mixes, w_outT]
        out_specs = [chan(d)] + out_specs
        out_shape = [sds((d, t), F32)] + out_shape
    in_specs += [full((d, 1)), _layer_weight_spec(w_inT, layer), full((HEAD_DIM, 1)), full((HEAD_DIM, 1)),
                 tab, tab, tab, tab, tab, tab]
    args += [ng_col, w_inT, qn_col, kn_col, *tables]
    return pl.pallas_call(
        functools.partial(_in_proj_kernel, token_major=token_major, after_out_proj=prev is not None),
        grid=(t // tm,),
        in_specs=in_specs,
        out_specs=out_specs,
        out_shape=out_shape,
        scratch_shapes=[pltpu.VMEM((d, tm), BF16)],
        compiler_params=_params(("parallel",)),
        name="in_proj",
    )(*args)


def _v_ext(v):
    return jnp.concatenate([v, jnp.ones((ONES_ROWS, v.shape[1]), v.dtype)], axis=0)


def _dense_kernel(*refs, n_q, diff, lam_init, seq):
    if diff:
        (q_ref, k_ref, v_ref, sg_ref, lam_ref, sub_ref, o_ref, s0_ref, s1_ref, acc_ref) = refs
    else:
        (q_ref, k_ref, v_ref, sg_ref, o_ref, s0_ref, s1_ref, acc_ref) = refs
    s_slots = (s0_ref, s1_ref)
    n_chunks = seq // TK
    n_qtiles = seq // TQ_DENSE
    trips_per_qtile = n_chunks // CHUNKS_PER_TRIP

    acc_ref[...] = jnp.zeros(acc_ref.shape, F32)

    tiles = [(i, slice(LANE_TILE * h, LANE_TILE * (h + 1))) for i in range(n_q) for h in range(TQ_DENSE // LANE_TILE)]

    def score_tile(slot, qtile, chunk, i, lanes):
        k = k_ref[pl.ds(pl.multiple_of(chunk * TK, TK), TK), :]
        q = q_ref[i, :, pl.ds(pl.multiple_of(qtile * TQ_DENSE + lanes.start, LANE_TILE), LANE_TILE)]
        s = jnp.dot(k, q, preferred_element_type=F32).astype(BF16)
        s_slots[slot][i, :, lanes] = s
        return jnp.max(s, axis=0, keepdims=True).astype(F32)

    def value_tile(slot, chunk, i, lanes, m_old, m_new):
        v = _v_ext(v_ref[:, pl.ds(pl.multiple_of(chunk * TK, TK), TK)])
        alpha = jnp.exp2(m_old - m_new)
        p = jnp.exp2(s_slots[slot][i, :, lanes] - m_new.astype(BF16))
        acc_ref[i, :, lanes] = alpha * acc_ref[i, :, lanes] + jnp.dot(v, p, preferred_element_type=F32)

    def normalized(i):
        acc = acc_ref[i]
        return acc[0:HEAD_DIM] / acc[HEAD_DIM:HEAD_DIM + 1]

    def finalize(qtile):
        cols = pl.ds(pl.multiple_of(qtile * TQ_DENSE, TQ_DENSE), TQ_DENSE)
        if diff:
            lv = lam_ref[...]
            lam = (jnp.exp(jnp.sum(lv[0:1] * lv[1:2], axis=1, keepdims=True))
                   - jnp.exp(jnp.sum(lv[2:3] * lv[3:4], axis=1, keepdims=True)) + lam_init)
            o = normalized(0) - lam * normalized(1)
            ms = jnp.mean(o * o, axis=0, keepdims=True)
            o = o * lax.rsqrt(ms + EPS) * sub_ref[...] * (1.0 - lam_init)
            o_ref[:, cols] = (o * sg_ref[:, cols].astype(F32)).astype(BF16)
        else:
            for i in range(n_q):
                r = slice(HEAD_DIM * i, HEAD_DIM * (i + 1))
                o_ref[r, cols] = (normalized(i) * sg_ref[r, cols].astype(F32)).astype(BF16)

    def body(jj, carry):
        m_old, m_run = carry
        qtile = jj // trips_per_qtile
        c0 = (jj % trips_per_qtile) * CHUNKS_PER_TRIP
        wraps = c0 + CHUNKS_PER_TRIP == n_chunks
        for u in range(CHUNKS_PER_TRIP):
            last = u == CHUNKS_PER_TRIP - 1
            q_next = jnp.minimum(jnp.where(wraps, qtile + 1, qtile), n_qtiles - 1) if last else qtile
            c_next = jnp.where(wraps, 0, c0 + u + 1) if last else c0 + u + 1
            nxt = []
            for t, (i, lanes) in enumerate(tiles):
                nxt.append(score_tile((u + 1) % 2, q_next, c_next, i, lanes))
                value_tile(u % 2, c0 + u, i, lanes, m_old[t], m_run[t])
            if last:
                m_old = [jnp.where(wraps, -jnp.inf, a) for a in m_run]
                m_run = [jnp.where(wraps, b, jnp.maximum(a, b)) for a, b in zip(m_run, nxt)]
            else:
                m_old, m_run = m_run, [jnp.maximum(a, b) for a, b in zip(m_run, nxt)]

        @pl.when(wraps)
        def _():
            finalize(qtile)

        return m_old, m_run

    first = [score_tile(0, 0, 0, i, lanes) for i, lanes in tiles]
    minus_inf = [jnp.full((1, LANE_TILE), -jnp.inf, F32)] * len(tiles)
    lax.fori_loop(0, n_qtiles * trips_per_qtile, body, (minus_inf, first))


def _dense_attn(q_pad, k_nat, vT, sgT, sg_row0, batch, seq, diff, lam_vecs=None, subln_col=None,
                lam_init=0.0):
    n_q = 2
    out_rows = HEAD_DIM if diff else 2 * HEAD_DIM
    n_groups = BRANCH_W // out_rows
    sg_blk0 = sg_row0 // out_rows
    in_specs = [
        pl.BlockSpec((n_q, KV_PAD, seq), lambda b, g: (g, 0, b)),
        pl.BlockSpec((seq, KV_PAD), lambda b, g: (b, 0)),
        pl.BlockSpec((HEAD_DIM, seq), (lambda b, g: (g // 2, b)) if diff else (lambda b, g: (g, b))),
        pl.BlockSpec((out_rows, seq), lambda b, g: (sg_blk0 + g, b)),
    ]
    args = [q_pad, k_nat, vT, sgT]
    if diff:
        in_specs += [pl.BlockSpec((4, DIFF_DIM), lambda b, g: (0, 0)),
                     pl.BlockSpec((HEAD_DIM, 1), lambda b, g: (0, 0))]
        args += [lam_vecs, subln_col]
    return pl.pallas_call(
        functools.partial(_dense_kernel, n_q=n_q, diff=diff, lam_init=lam_init, seq=seq),
        grid=(batch, n_groups),
        in_specs=in_specs,
        out_specs=pl.BlockSpec((out_rows, seq), lambda b, g: (g, b)),
        out_shape=jax.ShapeDtypeStruct((BRANCH_W, batch * seq), BF16),
        scratch_shapes=[pltpu.VMEM((n_q, TK, TQ_DENSE), BF16), pltpu.VMEM((n_q, TK, TQ_DENSE), BF16),
                        pltpu.VMEM((n_q, V_EXT, TQ_DENSE), F32)],
        compiler_params=_params(("parallel", "parallel")),
        name="diff_attn" if diff else "dense_attn",
    )(*args)


def _softmax_value_tile(sb, v, extra_logit=None):
    m = jnp.max(sb, axis=0, keepdims=True)
    if extra_logit is not None:
        m = jnp.maximum(m.astype(F32), extra_logit).astype(BF16)
    p = jnp.exp2(sb - m)
    acc = jnp.dot(v, p, preferred_element_type=F32)
    denom = acc[HEAD_DIM:HEAD_DIM + 1]
    if extra_logit is not None:
        denom = denom + jnp.exp2(extra_logit - m.astype(F32))
    return acc[0:HEAD_DIM], denom


def _two_stage_trips(n_trips, n_chains, score, value, slots):
    for c in range(n_chains):
        slots[0][c] = score(0, c)

    def body(jj, carry):
        for u in range(2):
            trip = 2 * jj + u
            nxt = jnp.minimum(trip + 1, n_trips - 1)
            for c in range(n_chains):
                slots[(u + 1) % 2][c] = score(nxt, c)
                value(trip, c, slots[u % 2][c])
        return carry

    lax.fori_loop(0, n_trips // 2, body, 0)


def _window_kernel(sink_ref, q_ref, k_ref, v_ref, sg_ref, o_ref, mask_ref, s0_ref, s1_ref, *, seq):
    g = pl.program_id(1)
    shape = (WIN_KEYS, LANE_TILE)
    rel = lax.broadcasted_iota(jnp.int32, shape, 0) - lax.broadcasted_iota(jnp.int32, shape, 1)
    for variant in range(WIN_MASKS):
        mask_ref[variant] = jnp.where(jnp.abs(rel - variant * WINDOW) <= WINDOW, 0.0, NEG_INF).astype(BF16)

    def geometry(trip, chain):
        qs = pl.multiple_of(trip * TQ + LANE_TILE * (chain // 2), LANE_TILE)
        start = pl.multiple_of(jnp.clip(qs - WINDOW, 0, seq - WIN_KEYS), WINDOW)
        return qs, start

    def score(trip, chain):
        qs, start = geometry(trip, chain)
        k = k_ref[pl.ds(start, WIN_KEYS), :]
        s = jnp.dot(k, q_ref[chain % 2, :, pl.ds(qs, LANE_TILE)], preferred_element_type=F32)
        return s.astype(BF16) + mask_ref[(qs - start) // WINDOW]

    def value(trip, chain, sb):
        qs, start = geometry(trip, chain)
        i = chain % 2
        rows, cols = slice(HEAD_DIM * i, HEAD_DIM * (i + 1)), pl.ds(qs, LANE_TILE)
        v = _v_ext(v_ref[:, pl.ds(start, WIN_KEYS)])
        out, denom = _softmax_value_tile(sb, v, sink_ref[2 * g + i] * LOG2E)
        o_ref[rows, cols] = (out / denom * sg_ref[rows, cols].astype(F32)).astype(BF16)

    _two_stage_trips(seq // TQ, 2 * (TQ // LANE_TILE), score, value, (s0_ref, s1_ref))


def _window_attn(sink, q_pad, k_nat, vT, sgT, sg_row0, batch, seq):
    sg_blk0 = sg_row0 // (2 * HEAD_DIM)
    return pl.pallas_call(
        functools.partial(_window_kernel, seq=seq),
        grid=(batch, N_HEADS // 2),
        in_specs=[
            pl.BlockSpec(memory_space=pltpu.SMEM),
            pl.BlockSpec((2, KV_PAD, seq), lambda b, g: (g, 0, b)),
            pl.BlockSpec((seq, KV_PAD), lambda b, g: (b, 0)),
            pl.BlockSpec((HEAD_DIM, seq), lambda b, g: (g, b)),
            pl.BlockSpec((2 * HEAD_DIM, seq), lambda b, g: (sg_blk0 + g, b)),
        ],
        out_specs=pl.BlockSpec((2 * HEAD_DIM, seq), lambda b, g: (g, b)),
        out_shape=jax.ShapeDtypeStruct((BRANCH_W, batch * seq), BF16),
        scratch_shapes=[pltpu.VMEM((WIN_MASKS, WIN_KEYS, LANE_TILE), BF16)]
        + [pltpu.VMEM((2 * (TQ // LANE_TILE), WIN_KEYS, LANE_TILE), BF16)] * 2,
        compiler_params=_params(("parallel", "parallel")),
        name="window_attn",
    )(sink, q_pad, k_nat, vT, sgT)


N_DR = 2 * NA_KH - 1
N_DC = 2 * NA_KW - 1


def _nbr_start_row(first_row, rows, clip):
    return clip(first_row - NA_KH // 2, 0, rows - NBR_KEY_ROWS)


def _nbr_kernel(q_ref, k_ref, v_ref, bias_ref, sg_ref, o_ref, s0_ref, s1_ref, *, seq):
    rows = seq // GRID_W

    def geometry(trip, chain):
        first_row = trip * (TQ // GRID_W) + NBR_TILE_ROWS * (chain // 2)
        cols = pl.ds(pl.multiple_of(first_row * GRID_W, LANE_TILE), LANE_TILE)
        start = _nbr_start_row(first_row, rows, jnp.clip) * GRID_W
        return first_row, cols, pl.multiple_of(start, NBR_TILE_ROWS * GRID_W)

    def score(trip, chain):
        first_row, cols, start = geometry(trip, chain)
        variant = jnp.where(first_row == 0, 0, jnp.where(first_row == rows - NBR_TILE_ROWS, 2, 1))
        k = k_ref[pl.ds(start, NBR_KEYS), :]
        s = jnp.dot(k, q_ref[chain % 2, :, cols], preferred_element_type=F32)
        return s.astype(BF16) + bias_ref[variant, chain % 2]

    def value(trip, chain, sb):
        _, cols, start = geometry(trip, chain)
        hrows = slice(HEAD_DIM * (chain % 2), HEAD_DIM * (chain % 2 + 1))
        v = _v_ext(v_ref[hrows, pl.ds(start, NBR_KEYS)])
        out, denom = _softmax_value_tile(sb, v)
        o_ref[hrows, cols] = (out / denom * sg_ref[hrows, cols].astype(F32)).astype(BF16)

    _two_stage_trips(seq // TQ, 2 * (TQ // LANE_TILE), score, value, (s0_ref, s1_ref))


def _nbr_bias_kernel(rpb_ref, o_ref, *, seq):
    h = pl.program_id(0)
    rows = seq // GRID_W
    kh = min(NA_KH, rows)
    shape = (GRID_W, 2 * GRID_W)
    kc = lax.broadcasted_iota(jnp.int32, shape, 0)
    lane = lax.broadcasted_iota(jnp.int32, shape, 1)
    qc = lane % GRID_W
    dcol = jnp.clip(kc - qc, -(NA_KW - 1), NA_KW - 1) + NA_KW - 1
    cs = jnp.clip(qc - NA_KW // 2, 0, GRID_W - NA_KW)
    col_ok = (kc >= cs) & (kc < cs + NA_KW)
    hits = [dcol == j for j in range(N_DC)]
    toeplitz = []
    for dr in range(N_DR):
        t = jnp.zeros(shape, F32)
        for j in range(N_DC):
            t = jnp.where(hits[j], rpb_ref[(h * N_DR + dr) * N_DC + j], t)
        toeplitz.append(jnp.where(col_ok, t * LOG2E, NEG_INF))
    masked = jnp.full(shape, NEG_INF, F32)
    left = lane < GRID_W
    for vi, first_row in enumerate((0, NBR_TILE_ROWS, rows - NBR_TILE_ROWS)):
        start_row = int(_nbr_start_row(first_row, rows, np.clip))
        for kr in range(NBR_KEY_ROWS):
            krow = start_row + kr
            for qp in range(NBR_TILE_ROWS // 2):
                halves = []
                for qrow in (first_row + 2 * qp, first_row + 2 * qp + 1):
                    rs = int(np.clip(qrow - kh // 2, 0, rows - kh))
                    halves.append(toeplitz[krow - qrow + NA_KH - 1] if rs <= krow < rs + kh else masked)
                blk = halves[0] if halves[0] is halves[1] else jnp.where(left, halves[0], halves[1])
                o_ref[vi, 0, GRID_W * kr:GRID_W * (kr + 1), 2 * GRID_W * qp:2 * GRID_W * (qp + 1)] = blk.astype(BF16)


def _nbr_bias_tables(rpb, seq):
    return pl.pallas_call(
        functools.partial(_nbr_bias_kernel, seq=seq),
        grid=(N_HEADS,),
        in_specs=[pl.BlockSpec(memory_space=pltpu.SMEM)],
        out_specs=pl.BlockSpec((3, 1, NBR_KEYS, LANE_TILE), lambda h: (0, h, 0, 0)),
        out_shape=jax.ShapeDtypeStruct((3, N_HEADS, NBR_KEYS, LANE_TILE), BF16),
        compiler_params=_params(("parallel",)),
        name="nbr_bias",
    )(rpb.astype(F32).reshape(-1))


def _nbr_attn(q_pad, k_nat, vT, bias, sgT, sg_row0, batch, seq):
    sg_blk0 = sg_row0 // (2 * HEAD_DIM)
    return pl.pallas_call(
        functools.partial(_nbr_kernel, seq=seq),
        grid=(batch, N_HEADS // 2),
        in_specs=[
            pl.BlockSpec((2, KV_PAD, seq), lambda b, g: (g, 0, b)),
            pl.BlockSpec((seq, KV_PAD), lambda b, g: (b, g)),
            pl.BlockSpec((2 * HEAD_DIM, seq), lambda b, g: (g, b)),
            pl.BlockSpec((3, 2, NBR_KEYS, LANE_TILE), lambda b, g: (0, g, 0, 0)),
            pl.BlockSpec((2 * HEAD_DIM, seq), lambda b, g: (sg_blk0 + g, b)),
        ],
        out_specs=pl.BlockSpec((2 * HEAD_DIM, seq), lambda b, g: (g, b)),
        out_shape=jax.ShapeDtypeStruct((BRANCH_W, batch * seq), BF16),
        scratch_shapes=[pltpu.VMEM((2 * (TQ // LANE_TILE), NBR_KEYS, LANE_TILE), BF16)] * 2,
        compiler_params=_params(("parallel", "parallel")),
        name="nbr_attn",
    )(q_pad, k_nat, vT, bias, sgT)


def _out_proj_kernel(x_ref, ma_ref, mb_ref, mc_ref, md_ref, w_ref, fg_ref, o_ref, *, token_major):
    acc = x_ref[...].T if token_major else x_ref[...]
    acc = _residual_update(acc, (ma_ref, mb_ref, mc_ref, md_ref), w_ref)
    ms = jnp.mean(acc * acc, axis=0, keepdims=True)
    o_ref[...] = (acc * lax.rsqrt(ms + EPS) * fg_ref[...]).T


def _out_proj_final(x, mixes, w_outT, layer, token_major, final_g_col):
    t, d = x.shape if token_major else x.shape[::-1]
    tm = TM_OUT_PROJ
    chan = lambda w: pl.BlockSpec((w, tm), lambda i: (0, i))
    nat = pl.BlockSpec((tm, d), lambda i: (i, 0))
    return pl.pallas_call(
        functools.partial(_out_proj_kernel, token_major=token_major),
        grid=(t // tm,),
        in_specs=([nat if token_major else chan(d)] + [chan(BRANCH_W)] * 4
                  + [_layer_weight_spec(w_outT, layer), pl.BlockSpec((d, 1), lambda i: (0, 0))]),
        out_specs=nat,
        out_shape=jax.ShapeDtypeStruct((t, d), F32),
        compiler_params=_params(("parallel",)),
        name="out_proj_final",
    )(x, *mixes, w_outT, final_g_col)


def _rope_tables(seq):
    t = jnp.arange(seq)
    row, col = t // GRID_W, t % GRID_W

    def cs(pos, d):
        inv = jnp.power(ROPE_THETA, -jnp.arange(0, d, 2, dtype=F32) / d)
        ang = pos.astype(F32)[None, :] * inv[:, None]
        return jnp.cos(ang), jnp.sin(ang)

    cr, sr = cs(row, HEAD_DIM // 2)
    cc, sc = cs(col, HEAD_DIM // 2)
    ct, st = cs(t, HEAD_DIM)
    cdt, sdt = cs(t, DIFF_DIM)
    cat = lambda *a: jnp.concatenate(a, axis=0)
    return (cat(cr, cr, cc, cc), cat(-sr, sr, -sc, sc),
            cat(ct, ct), cat(-st, st),
            cat(cdt, cdt, cdt, cdt), cat(-sdt, sdt, -sdt, sdt))


def kernel(x, norm_g, w_in, w_out, qn_a, kn_a, sink_b, rpb_c, lam_q1, lam_k1, lam_q2, lam_k2, subln_d, final_g):
    batch, seq, d = x.shape
    depth = w_in.shape[0]
    tables = _rope_tables(seq)
    res = x.reshape(batch * seq, d)
    token_major = True
    pending = None
    w_inT = _transposed_bf16(w_in)
    w_outT = _transposed_bf16(w_out)
    for l in range(depth):
        outs = _in_proj(res, norm_g[l].reshape(d, 1), w_inT, l, qn_a[l].reshape(HEAD_DIM, 1),
                        kn_a[l].reshape(HEAD_DIM, 1), tables, seq, token_major, pending)
        if pending is not None:
            res, token_major = outs[0], False
            outs = outs[1:]
        (qa, ka, va, qb, kb, vb, qc, kc, vc, qd, kd, vd, sg) = outs
        mix_a = _dense_attn(qa, ka, va, sg, 0, batch, seq, diff=False)
        mix_b = _window_attn(sink_b[l], qb, kb, vb, sg, BRANCH_W, batch, seq)
        mix_c = _nbr_attn(qc, kc, vc, _nbr_bias_tables(rpb_c[l], seq), sg, 2 * BRANCH_W, batch, seq)
        lam_init = 0.8 - 0.6 * math.exp(-0.3 * l)
        lam_vecs = jnp.stack([lam_q1[l], lam_k1[l], lam_q2[l], lam_k2[l]]).astype(F32)
        mix_d = _dense_attn(qd, kd, vd, sg, 3 * BRANCH_W, batch, seq, diff=True, lam_vecs=lam_vecs,
                            subln_col=subln_d[l].reshape(HEAD_DIM, 1), lam_init=lam_init)
        pending = ((mix_a, mix_b, mix_c, mix_d), w_outT)
    out = _out_proj_final(res, *pending, depth - 1, token_major, final_g.reshape(d, 1))
    return out.reshape(batch, seq, d)
```

```python
import functools
import math

import jax
import jax.numpy as jnp
import numpy as np
from jax import lax
from jax.experimental import pallas as pl
from jax.experimental.pallas import tpu as pltpu

HEAD_DIM = 64
GRID_W = 64
DIFF_DIM = 32
WINDOW = 128
NA_KH = 8
NA_KW = 16
ROPE_THETA = 10000.0
EPS = 1e-6
NEG_INF = -1e30
LOG2E = math.log2(math.e)

N_HEADS = 4
BRANCH_W = N_HEADS * HEAD_DIM
KV_PAD = 128
ONES_ROWS = 16
V_EXT = HEAD_DIM + ONES_ROWS

V7X_VMEM_BYTES = 64 * 1024 * 1024
VMEM_LIMIT = V7X_VMEM_BYTES * 3 // 4

TM_PROJ = 512
WEIGHT_PREP_STEPS = 2
TM_OUT_PROJ = 512
TQ = 1024
TQ_DENSE = 512
TK = 512
CHUNKS_PER_TRIP = 8
LANE_TILE = 256
WIN_KEYS = LANE_TILE + 2 * WINDOW
WIN_MASKS = (WIN_KEYS - LANE_TILE) // WINDOW + 1
NBR_TILE_ROWS = LANE_TILE // GRID_W
NBR_KEY_ROWS = NBR_TILE_ROWS + NA_KH
NBR_KEYS = NBR_KEY_ROWS * GRID_W

F32 = jnp.float32
BF16 = jnp.bfloat16


def _params(sem):
    return pltpu.CompilerParams(dimension_semantics=sem, vmem_limit_bytes=VMEM_LIMIT)


def _weight_prep_kernel(w_ref, o_ref):
    o_ref[0] = w_ref[0].T.astype(BF16)


def _transposed_bf16(w):
    depth, k, n = w.shape
    tn = n // WEIGHT_PREP_STEPS
    return pl.pallas_call(
        _weight_prep_kernel,
        grid=(depth, n // tn),
        in_specs=[pl.BlockSpec((1, k, tn), lambda l, j: (l, 0, j))],
        out_specs=pl.BlockSpec((1, tn, k), lambda l, j: (l, j, 0)),
        out_shape=jax.ShapeDtypeStruct((depth, n, k), BF16),
        compiler_params=_params(("parallel", "parallel")),
        name="weight_prep",
    )(w)


def _layer_weight_spec(stacked, layer):
    return pl.BlockSpec((None,) + stacked.shape[1:], lambda *_: (layer, 0, 0))


def _rot_quarters(y):
    return jnp.concatenate([y[16:32], y[0:16], y[48:64], y[32:48]], axis=0)


def _rot_halves(y):
    return jnp.concatenate([y[32:64], y[0:32]], axis=0)


def _head_rms(y, g_col):
    ms = jnp.mean(y * y, axis=0, keepdims=True)
    return y * lax.rsqrt(ms + EPS) * g_col


def _pad_rows(y, slot, n_slots):
    z = jnp.zeros_like(y)
    return jnp.concatenate([y if s == slot else z for s in range(n_slots)], axis=0)


def _silu(g):
    return g * (1.0 / (1.0 + jnp.exp(-g)))


def _residual_update(x, mix_refs, wo_ref):
    for j, m_ref in enumerate(mix_refs):
        x = x + jnp.dot(wo_ref[:, BRANCH_W * j:BRANCH_W * (j + 1)], m_ref[...], preferred_element_type=F32)
    return x


def _in_proj_kernel(*refs, token_major, after_out_proj):
    x_ref, refs = refs[0], refs[1:]
    if after_out_proj:
        mix_refs, wo_ref, refs = refs[0:4], refs[4], refs[5:]
    (ng_ref, w_ref, qn_ref, kn_ref, ca_ref, sa_ref, cb_ref, sb_ref, cd_ref, sd_ref), refs = refs[:10], refs[10:]
    if after_out_proj:
        res_ref, refs = refs[0], refs[1:]
    (qa_ref, ka_ref, va_ref, qb_ref, kb_ref, vb_ref,
     qc_ref, kc_ref, vc_ref, qd_ref, kd_ref, vd_ref, sg_ref, h_ref) = refs

    x = x_ref[...].T if token_major else x_ref[...]
    if after_out_proj:
        x = _residual_update(x, mix_refs, wo_ref)
        res_ref[...] = x
    ms = jnp.mean(x * x, axis=0, keepdims=True)
    h_ref[...] = (x * lax.rsqrt(ms + EPS) * ng_ref[...]).astype(BF16)

    def proj(r0, nrows):
        return jnp.dot(w_ref[r0:r0 + nrows, :], h_ref[...], preferred_element_type=F32)

    sc_hd = HEAD_DIM ** -0.5
    ca, sa = ca_ref[...], sa_ref[...]
    cb, sb = cb_ref[...], sb_ref[...]
    cd, sd = cd_ref[...], sd_ref[...]
    qn, kn = qn_ref[...], kn_ref[...]

    def rope_a(y):
        return y * ca + _rot_quarters(y) * sa

    def rope_b(y):
        return y * cb + _rot_halves(y) * sb

    def rope_d(y):
        return y * cd + _rot_quarters(y) * sd

    base = 0
    z = proj(base, 256)
    for hh in range(N_HEADS):
        y = rope_a(_head_rms(z[64 * hh:64 * hh + 64], qn)) * (sc_hd * LOG2E)
        qa_ref[hh] = _pad_rows(y, hh // 2, 2).astype(BF16)
    z = proj(base + 256, 128)
    kk = jnp.concatenate([rope_a(_head_rms(z[64 * j:64 * j + 64], kn)) for j in range(2)], axis=0)
    ka_ref[...] = kk.T.astype(BF16)
    va_ref[...] = proj(base + 384, 128).astype(BF16)
    sg_ref[0:256, :] = _silu(proj(base + 512, 256)).astype(BF16)

    base = 768
    z = proj(base, 256)
    for hh in range(N_HEADS):
        y = rope_b(z[64 * hh:64 * hh + 64]) * (sc_hd * LOG2E)
        qb_ref[hh] = _pad_rows(y, hh // 2, 2).astype(BF16)
    z = proj(base + 256, 128)
    kk = jnp.concatenate([rope_b(z[64 * j:64 * j + 64]) for j in range(2)], axis=0)
    kb_ref[...] = kk.T.astype(BF16)
    vb_ref[...] = proj(base + 384, 128).astype(BF16)
    sg_ref[256:512, :] = _silu(proj(base + 512, 256)).astype(BF16)

    base = 1536
    z = proj(base, 256)
    for hh in range(N_HEADS):
        y = z[64 * hh:64 * hh + 64] * (sc_hd * LOG2E)
        qc_ref[hh] = _pad_rows(y, hh % 2, 2).astype(BF16)
    z = proj(base + 256, 256)
    kc_ref[...] = z.T.astype(BF16)
    vc_ref[...] = proj(base + 512, 256).astype(BF16)
    sg_ref[512:768, :] = _silu(proj(base + 768, 256)).astype(BF16)

    base = 2560
    z = proj(base, 256)
    sc_dd = DIFF_DIM ** -0.5
    for hh in range(N_HEADS):
        y = rope_d(z[64 * hh:64 * hh + 64]) * (sc_dd * LOG2E)
        for c in range(2):
            qd_ref[2 * hh + c] = _pad_rows(y[32 * c:32 * c + 32], 2 * (hh // 2) + c, 4).astype(BF16)
    z = proj(base + 256, 128)
    kk = jnp.concatenate([rope_d(z[64 * j:64 * j + 64]) for j in range(2)], axis=0)
    kd_ref[...] = kk.T.astype(BF16)
    sg_ref[768:1024, :] = _silu(proj(base + 512, 256)).astype(BF16)
    vd_ref[...] = proj(base + 384, 128).astype(BF16)


def _in_proj(x, ng_col, w_inT, layer, qn_col, kn_col, tables, seq, token_major, prev=None):
    t, d = x.shape if token_major else x.shape[::-1]
    tm = TM_PROJ
    nt_seq = seq // tm
    full = lambda shape: pl.BlockSpec(shape, lambda i: (0,) * len(shape))
    tab = pl.BlockSpec((HEAD_DIM, tm), lambda i: (0, i % nt_seq))
    qpad = lambda n: pl.BlockSpec((n, KV_PAD, tm), lambda i: (0, 0, i))
    nat = lambda w: pl.BlockSpec((tm, w), lambda i: (i, 0))
    chan = lambda w: pl.BlockSpec((w, tm), lambda i: (0, i))
    sds = jax.ShapeDtypeStruct
    out_shape = [
        sds((N_HEADS, KV_PAD, t), BF16), sds((t, 128), BF16), sds((128, t), BF16),
        sds((N_HEADS, KV_PAD, t), BF16), sds((t, 128), BF16), sds((128, t), BF16),
        sds((N_HEADS, KV_PAD, t), BF16), sds((t, 256), BF16), sds((256, t), BF16),
        sds((2 * N_HEADS, KV_PAD, t), BF16), sds((t, 128), BF16), sds((128, t), BF16),
        sds((4 * BRANCH_W, t), BF16),
    ]
    out_specs = [
        qpad(N_HEADS), nat(128), chan(128),
        qpad(N_HEADS), nat(128), chan(128),
        qpad(N_HEADS), nat(256), chan(256),
        qpad(2 * N_HEADS), nat(128), chan(128),
        chan(4 * BRANCH_W),
    ]
    in_specs = [nat(d) if token_major else chan(d)]
    args = [x]
    if prev is not None:
        mixes, w_outT = prev
        in_specs += [chan(BRANCH_W)] * len(mixes) + [_layer_weight_spec(w_outT, layer - 1)]
        args += [*mixes, w_outT]
        out_specs = [chan(d)] + out_specs
        out_shape = [sds((d, t), F32)] + out_shape
    in_specs += [full((d, 1)), _layer_weight_spec(w_inT, layer), full((HEAD_DIM, 1)), full((HEAD_DIM, 1)),
                 tab, tab, tab, tab, tab, tab]
    args += [ng_col, w_inT, qn_col, kn_col, *tables]
    return pl.pallas_call(
        functools.partial(_in_proj_kernel, token_major=token_major, after_out_proj=prev is not None),
        grid=(t // tm,),
        in_specs=in_specs,
        out_specs=out_specs,
        out_shape=out_shape,
        scratch_shapes=[pltpu.VMEM((d, tm), BF16)],
        compiler_params=_params(("parallel",)),
        name="in_proj",
    )(*args)


def _v_ext(v):
    return jnp.concatenate([v, jnp.ones((ONES_ROWS, v.shape[1]), v.dtype)], axis=0)


def _dense_kernel(*refs, n_q, diff, lam_init, seq):
    if diff:
        (q_ref, k_ref, v_ref, sg_ref, lam_ref, sub_ref, o_ref, s0_ref, s1_ref, acc_ref) = refs
    else:
        (q_ref, k_ref, v_ref, sg_ref, o_ref, s0_ref, s1_ref, acc_ref) = refs
    s_slots = (s0_ref, s1_ref)
    n_chunks = seq // TK
    n_qtiles = seq // TQ_DENSE
    trips_per_qtile = n_chunks // CHUNKS_PER_TRIP

    acc_ref[...] = jnp.zeros(acc_ref.shape, F32)

    tiles = [(i, slice(LANE_TILE * h, LANE_TILE * (h + 1))) for i in range(n_q) for h in range(TQ_DENSE // LANE_TILE)]

    def score_tile(slot, qtile, chunk, i, lanes):
        k = k_ref[pl.ds(pl.multiple_of(chunk * TK, TK), TK), :]
        q = q_ref[i, :, pl.ds(pl.multiple_of(qtile * TQ_DENSE + lanes.start, LANE_TILE), LANE_TILE)]
        s = jnp.dot(k, q, preferred_element_type=F32).astype(BF16)
        s_slots[slot][i, :, lanes] = s
        return jnp.max(s, axis=0, keepdims=True).astype(F32)

    def value_tile(slot, chunk, i, lanes, m_old, m_new):
        v = _v_ext(v_ref[:, pl.ds(pl.multiple_of(chunk * TK, TK), TK)])
        alpha = jnp.exp2(m_old - m_new)
        p = jnp.exp2(s_slots[slot][i, :, lanes] - m_new.astype(BF16))
        acc_ref[i, :, lanes] = alpha * acc_ref[i, :, lanes] + jnp.dot(v, p, preferred_element_type=F32)

    def normalized(i):
        acc = acc_ref[i]
        return acc[0:HEAD_DIM] / acc[HEAD_DIM:HEAD_DIM + 1]

    def finalize(qtile):
        cols = pl.ds(pl.multiple_of(qtile * TQ_DENSE, TQ_DENSE), TQ_DENSE)
        if diff:
            lv = lam_ref[...]
            lam = (jnp.exp(jnp.sum(lv[0:1] * lv[1:2], axis=1, keepdims=True))
                   - jnp.exp(jnp.sum(lv[2:3] * lv[3:4], axis=1, keepdims=True)) + lam_init)
            o = normalized(0) - lam * normalized(1)
            ms = jnp.mean(o * o, axis=0, keepdims=True)
            o = o * lax.rsqrt(ms + EPS) * sub_ref[...] * (1.0 - lam_init)
            o_ref[:, cols] = (o * sg_ref[:, cols].astype(F32)).astype(BF16)
        else:
            for i in range(n_q):
                r = slice(HEAD_DIM * i, HEAD_DIM * (i + 1))
                o_ref[r, cols] = (normalized(i) * sg_ref[r, cols].astype(F32)).astype(BF16)

    def body(jj, carry):
        m_old, m_run = carry
        qtile = jj // trips_per_qtile
        c0 = (jj % trips_per_qtile) * CHUNKS_PER_TRIP
        wraps = c0 + CHUNKS_PER_TRIP == n_chunks
        for u in range(CHUNKS_PER_TRIP):
            last = u == CHUNKS_PER_TRIP - 1
            q_next = jnp.minimum(jnp.where(wraps, qtile + 1, qtile), n_qtiles - 1) if last else qtile
            c_next = jnp.where(wraps, 0, c0 + u + 1) if last else c0 + u + 1
            nxt = []
            for t, (i, lanes) in enumerate(tiles):
                nxt.append(score_tile((u + 1) % 2, q_next, c_next, i, lanes))
                value_tile(u % 2, c0 + u, i, lanes, m_old[t], m_run[t])
            if last:
                m_old = [jnp.where(wraps, -jnp.inf, a) for a in m_run]
                m_run = [jnp.where(wraps, b, jnp.maximum(a, b)) for a, b in zip(m_run, nxt)]
            else:
                m_old, m_run = m_run, [jnp.maximum(a, b) for a, b in zip(m_run, nxt)]

        @pl.when(wraps)
        def _():
            finalize(qtile)

        return m_old, m_run

    first = [score_tile(0, 0, 0, i, lanes) for i, lanes in tiles]
    minus_inf = [jnp.full((1, LANE_TILE), -jnp.inf, F32)] * len(tiles)
    lax.fori_loop(0, n_qtiles * trips_per_qtile, body, (minus_inf, first))


def _dense_attn(q_pad, k_nat, vT, sgT, sg_row0, batch, seq, diff, lam_vecs=None, subln_col=None,
                lam_init=0.0):
    n_q = 2
    out_rows = HEAD_DIM if diff else 2 * HEAD_DIM
    n_groups = BRANCH_W // out_rows
    sg_blk0 = sg_row0 // out_rows
    in_specs = [
        pl.BlockSpec((n_q, KV_PAD, seq), lambda b, g: (g, 0, b)),
        pl.BlockSpec((seq, KV_PAD), lambda b, g: (b, 0)),
        pl.BlockSpec((HEAD_DIM, seq), (lambda b, g: (g // 2, b)) if diff else (lambda b, g: (g, b))),
        pl.BlockSpec((out_rows, seq), lambda b, g: (sg_blk0 + g, b)),
    ]
    args = [q_pad, k_nat, vT, sgT]
    if diff:
        in_specs += [pl.BlockSpec((4, DIFF_DIM), lambda b, g: (0, 0)),
                     pl.BlockSpec((HEAD_DIM, 1), lambda b, g: (0, 0))]
        args += [lam_vecs, subln_col]
    return pl.pallas_call(
        functools.partial(_dense_kernel, n_q=n_q, diff=diff, lam_init=lam_init, seq=seq),
        grid=(batch, n_groups),
        in_specs=in_specs,
        out_specs=pl.BlockSpec((out_rows, seq), lambda b, g: (g, b)),
        out_shape=jax.ShapeDtypeStruct((BRANCH_W, batch * seq), BF16),
        scratch_shapes=[pltpu.VMEM((n_q, TK, TQ_DENSE), BF16), pltpu.VMEM((n_q, TK, TQ_DENSE), BF16),
                        pltpu.VMEM((n_q, V_EXT, TQ_DENSE), F32)],
        compiler_params=_params(("parallel", "parallel")),
        name="diff_attn" if diff else "dense_attn",
    )(*args)


def _softmax_value_tile(sb, v, extra_logit=None):
    m = jnp.max(sb, axis=0, keepdims=True)
    if extra_logit is not None:
        m = jnp.maximum(m.astype(F32), extra_logit).astype(BF16)
    p = jnp.exp2(sb - m)
    acc = jnp.dot(v, p, preferred_element_type=F32)
    denom = acc[HEAD_DIM:HEAD_DIM + 1]
    if extra_logit is not None:
        denom = denom + jnp.exp2(extra_logit - m.astype(F32))
    return acc[0:HEAD_DIM], denom


def _two_stage_trips(n_trips, n_chains, score, value, slots):
    for c in range(n_chains):
        slots[0][c] = score(0, c)

    def body(jj, carry):
        for u in range(2):
            trip = 2 * jj + u
            nxt = jnp.minimum(trip + 1, n_trips - 1)
            for c in range(n_chains):
                slots[(u + 1) % 2][c] = score(nxt, c)
                value(trip, c, slots[u % 2][c])
        return carry

    lax.fori_loop(0, n_trips // 2, body, 0)


def _window_kernel(sink_ref, q_ref, k_ref, v_ref, sg_ref, o_ref, mask_ref, s0_ref, s1_ref, *, seq):
    g = pl.program_id(1)
    shape = (WIN_KEYS, LANE_TILE)
    rel = lax.broadcasted_iota(jnp.int32, shape, 0) - lax.broadcasted_iota(jnp.int32, shape, 1)
    for variant in range(WIN_MASKS):
        mask_ref[variant] = jnp.where(jnp.abs(rel - variant * WINDOW) <= WINDOW, 0.0, NEG_INF).astype(BF16)

    def geometry(trip, chain):
        qs = pl.multiple_of(trip * TQ + LANE_TILE * (chain // 2), LANE_TILE)
        start = pl.multiple_of(jnp.clip(qs - WINDOW, 0, seq - WIN_KEYS), WINDOW)
        return qs, start

    def score(trip, chain):
        qs, start = geometry(trip, chain)
        k = k_ref[pl.ds(start, WIN_KEYS), :]
        s = jnp.dot(k, q_ref[chain % 2, :, pl.ds(qs, LANE_TILE)], preferred_element_type=F32)
        return s.astype(BF16) + mask_ref[(qs - start) // WINDOW]

    def value(trip, chain, sb):
        qs, start = geometry(trip, chain)
        i = chain % 2
        rows, cols = slice(HEAD_DIM * i, HEAD_DIM * (i + 1)), pl.ds(qs, LANE_TILE)
        v = _v_ext(v_ref[:, pl.ds(start, WIN_KEYS)])
        out, denom = _softmax_value_tile(sb, v, sink_ref[2 * g + i] * LOG2E)
        o_ref[rows, cols] = (out / denom * sg_ref[rows, cols].astype(F32)).astype(BF16)

    _two_stage_trips(seq // TQ, 2 * (TQ // LANE_TILE), score, value, (s0_ref, s1_ref))


def _window_attn(sink, q_pad, k_nat, vT, sgT, sg_row0, batch, seq):
    sg_blk0 = sg_row0 // (2 * HEAD_DIM)
    return pl.pallas_call(
        functools.partial(_window_kernel, seq=seq),
        grid=(batch, N_HEADS // 2),
        in_specs=[
            pl.BlockSpec(memory_space=pltpu.SMEM),
            pl.BlockSpec((2, KV_PAD, seq), lambda b, g: (g, 0, b)),
            pl.BlockSpec((seq, KV_PAD), lambda b, g: (b, 0)),
            pl.BlockSpec((HEAD_DIM, seq), lambda b, g: (g, b)),
            pl.BlockSpec((2 * HEAD_DIM, seq), lambda b, g: (sg_blk0 + g, b)),
        ],
        out_specs=pl.BlockSpec((2 * HEAD_DIM, seq), lambda b, g: (g, b)),
        out_shape=jax.ShapeDtypeStruct((BRANCH_W, batch * seq), BF16),
        scratch_shapes=[pltpu.VMEM((WIN_MASKS, WIN_KEYS, LANE_TILE), BF16)]
        + [pltpu.VMEM((2 * (TQ // LANE_TILE), WIN_KEYS, LANE_TILE), BF16)] * 2,
        compiler_params=_params(("parallel", "parallel")),
        name="window_attn",
    )(sink, q_pad, k_nat, vT, sgT)


N_DR = 2 * NA_KH - 1
N_DC = 2 * NA_KW - 1


def _nbr_start_row(first_row, rows, clip):
    return clip(first_row - NA_KH // 2, 0, rows - NBR_KEY_ROWS)


def _nbr_kernel(q_ref, k_ref, v_ref, bias_ref, sg_ref, o_ref, s0_ref, s1_ref, *, seq):
    rows = seq // GRID_W

    def geometry(trip, chain):
        first_row = trip * (TQ // GRID_W) + NBR_TILE_ROWS * (chain // 2)
        cols = pl.ds(pl.multiple_of(first_row * GRID_W, LANE_TILE), LANE_TILE)
        start = _nbr_start_row(first_row, rows, jnp.clip) * GRID_W
        return first_row, cols, pl.multiple_of(start, NBR_TILE_ROWS * GRID_W)

    def score(trip, chain):
        first_row, cols, start = geometry(trip, chain)
        variant = jnp.where(first_row == 0, 0, jnp.where(first_row == rows - NBR_TILE_ROWS, 2, 1))
        k = k_ref[pl.ds(start, NBR_KEYS), :]
        s = jnp.dot(k, q_ref[chain % 2, :, cols], preferred_element_type=F32)
        return s.astype(BF16) + bias_ref[variant, chain % 2]

    def value(trip, chain, sb):
        _, cols, start = geometry(trip, chain)
        hrows = slice(HEAD_DIM * (chain % 2), HEAD_DIM * (chain % 2 + 1))
        v = _v_ext(v_ref[hrows, pl.ds(start, NBR_KEYS)])
        out, denom = _softmax_value_tile(sb, v)
        o_ref[hrows, cols] = (out / denom * sg_ref[hrows, cols].astype(F32)).astype(BF16)

    _two_stage_trips(seq // TQ, 2 * (TQ // LANE_TILE), score, value, (s0_ref, s1_ref))


def _nbr_bias_kernel(rpb_ref, o_ref, *, seq):
    h = pl.program_id(0)
    rows = seq // GRID_W
    kh = min(NA_KH, rows)
    shape = (GRID_W, 2 * GRID_W)
    kc = lax.broadcasted_iota(jnp.int32, shape, 0)
    lane = lax.broadcasted_iota(jnp.int32, shape, 1)
    qc = lane % GRID_W
    dcol = jnp.clip(kc - qc, -(NA_KW - 1), NA_KW - 1) + NA_KW - 1
    cs = jnp.clip(qc - NA_KW // 2, 0, GRID_W - NA_KW)
    col_ok = (kc >= cs) & (kc < cs + NA_KW)
    hits = [dcol == j for j in range(N_DC)]
    toeplitz = []
    for dr in range(N_DR):
        t = jnp.zeros(shape, F32)
        for j in range(N_DC):
            t = jnp.where(hits[j], rpb_ref[(h * N_DR + dr) * N_DC + j], t)
        toeplitz.append(jnp.where(col_ok, t * LOG2E, NEG_INF))
    masked = jnp.full(shape, NEG_INF, F32)
    left = lane < GRID_W
    for vi, first_row in enumerate((0, NBR_TILE_ROWS, rows - NBR_TILE_ROWS)):
        start_row = int(_nbr_start_row(first_row, rows, np.clip))
        for kr in range(NBR_KEY_ROWS):
            krow = start_row + kr
            for qp in range(NBR_TILE_ROWS // 2):
                halves = []
                for qrow in (first_row + 2 * qp, first_row + 2 * qp + 1):
                    rs = int(np.clip(qrow - kh // 2, 0, rows - kh))
                    halves.append(toeplitz[krow - qrow + NA_KH - 1] if rs <= krow < rs + kh else masked)
                blk = halves[0] if halves[0] is halves[1] else jnp.where(left, halves[0], halves[1])
                o_ref[vi, 0, GRID_W * kr:GRID_W * (kr + 1), 2 * GRID_W * qp:2 * GRID_W * (qp + 1)] = blk.astype(BF16)


def _nbr_bias_tables(rpb, seq):
    return pl.pallas_call(
        functools.partial(_nbr_bias_kernel, seq=seq),
        grid=(N_HEADS,),
        in_specs=[pl.BlockSpec(memory_space=pltpu.SMEM)],
        out_specs=pl.BlockSpec((3, 1, NBR_KEYS, LANE_TILE), lambda h: (0, h, 0, 0)),
        out_shape=jax.ShapeDtypeStruct((3, N_HEADS, NBR_KEYS, LANE_TILE), BF16),
        compiler_params=_params(("parallel",)),
        name="nbr_bias",
    )(rpb.astype(F32).reshape(-1))


def _nbr_attn(q_pad, k_nat, vT, bias, sgT, sg_row0, batch, seq):
    sg_blk0 = sg_row0 // (2 * HEAD_DIM)
    return pl.pallas_call(
        functools.partial(_nbr_kernel, seq=seq),
        grid=(batch, N_HEADS // 2),
        in_specs=[
            pl.BlockSpec((2, KV_PAD, seq), lambda b, g: (g, 0, b)),
            pl.BlockSpec((seq, KV_PAD), lambda b, g: (b, g)),
            pl.BlockSpec((2 * HEAD_DIM, seq), lambda b, g: (g, b)),
            pl.BlockSpec((3, 2, NBR_KEYS, LANE_TILE), lambda b, g: (0, g, 0, 0)),
            pl.BlockSpec((2 * HEAD_DIM, seq), lambda b, g: (sg_blk0 + g, b)),
        ],
        out_specs=pl.BlockSpec((2 * HEAD_DIM, seq), lambda b, g: (g, b)),
        out_shape=jax.ShapeDtypeStruct((BRANCH_W, batch * seq), BF16),
        scratch_shapes=[pltpu.VMEM((2 * (TQ // LANE_TILE), NBR_KEYS, LANE_TILE), BF16)] * 2,
        compiler_params=_params(("parallel", "parallel")),
        name="nbr_attn",
    )(q_pad, k_nat, vT, bias, sgT)


def _out_proj_kernel(x_ref, ma_ref, mb_ref, mc_ref, md_ref, w_ref, fg_ref, o_ref, *, token_major):
    acc = x_ref[...].T if token_major else x_ref[...]
    acc = _residual_update(acc, (ma_ref, mb_ref, mc_ref, md_ref), w_ref)
    ms = jnp.mean(acc * acc, axis=0, keepdims=True)
    o_ref[...] = (acc * lax.rsqrt(ms + EPS) * fg_ref[...]).T


def _out_proj_final(x, mixes, w_outT, layer, token_major, final_g_col):
    t, d = x.shape if token_major else x.shape[::-1]
    tm = TM_OUT_PROJ
    chan = lambda w: pl.BlockSpec((w, tm), lambda i: (0, i))
    nat = pl.BlockSpec((tm, d), lambda i: (i, 0))
    return pl.pallas_call(
        functools.partial(_out_proj_kernel, token_major=token_major),
        grid=(t // tm,),
        in_specs=([nat if token_major else chan(d)] + [chan(BRANCH_W)] * 4
                  + [_layer_weight_spec(w_outT, layer), pl.BlockSpec((d, 1), lambda i: (0, 0))]),
        out_specs=nat,
        out_shape=jax.ShapeDtypeStruct((t, d), F32),
        compiler_params=_params(("parallel",)),
        name="out_proj_final",
    )(x, *mixes, w_outT, final_g_col)


def _rope_tables(seq):
    t = jnp.arange(seq)
    row, col = t // GRID_W, t % GRID_W

    def cs(pos, d):
        inv = jnp.power(ROPE_THETA, -jnp.arange(0, d, 2, dtype=F32) / d)
        ang = pos.astype(F32)[None, :] * inv[:, None]
        return jnp.cos(ang), jnp.sin(ang)

    cr, sr = cs(row, HEAD_DIM // 2)
    cc, sc = cs(col, HEAD_DIM // 2)
    ct, st = cs(t, HEAD_DIM)
    cdt, sdt = cs(t, DIFF_DIM)
    cat = lambda *a: jnp.concatenate(a, axis=0)
    return (cat(cr, cr, cc, cc), cat(-sr, sr, -sc, sc),
            cat(ct, ct), cat(-st, st),
            cat(cdt, cdt, cdt, cdt), cat(-sdt, sdt, -sdt, sdt))


def _check_shapes(x, w_in, w_out):
    batch, seq, d = x.shape
    depth = w_in.shape[0]
    d_in = 13 * BRANCH_W
    assert w_in.shape == (depth, d, d_in) and w_out.shape == (depth, 4 * BRANCH_W, d), (w_in.shape, w_out.shape)
    assert d % 128 == 0 and seq % GRID_W == 0 and seq % TM_PROJ == 0 and seq % TM_OUT_PROJ == 0, (d, seq)
    assert (d_in // WEIGHT_PREP_STEPS) % 128 == 0 and (d // WEIGHT_PREP_STEPS) % 128 == 0
    assert seq % (TK * CHUNKS_PER_TRIP) == 0 and seq % TQ_DENSE == 0 and seq % (2 * TQ) == 0, seq
    assert seq >= WIN_KEYS and seq // GRID_W >= NBR_KEY_ROWS + NBR_TILE_ROWS, seq


def kernel(x, norm_g, w_in, w_out, qn_a, kn_a, sink_b, rpb_c, lam_q1, lam_k1, lam_q2, lam_k2, subln_d, final_g):
    _check_shapes(x, w_in, w_out)
    batch, seq, d = x.shape
    depth = w_in.shape[0]
    tables = _rope_tables(seq)
    res = x.reshape(batch * seq, d)
    token_major = True
    pending = None
    w_inT = _transposed_bf16(w_in)
    w_outT = _transposed_bf16(w_out)
    for l in range(depth):
        outs = _in_proj(res, norm_g[l].reshape(d, 1), w_inT, l, qn_a[l].reshape(HEAD_DIM, 1),
                        kn_a[l].reshape(HEAD_DIM, 1), tables, seq, token_major, pending)
        if pending is not None:
            res, token_major = outs[0], False
            outs = outs[1:]
        (qa, ka, va, qb, kb, vb, qc, kc, vc, qd, kd, vd, sg) = outs
        mix_a = _dense_attn(qa, ka, va, sg, 0, batch, seq, diff=False)
        mix_b = _window_attn(sink_b[l], qb, kb, vb, sg, BRANCH_W, batch, seq)
        mix_c = _nbr_attn(qc, kc, vc, _nbr_bias_tables(rpb_c[l], seq), sg, 2 * BRANCH_W, batch, seq)
        lam_init = 0.8 - 0.6 * math.exp(-0.3 * l)
        lam_vecs = jnp.stack([lam_q1[l], lam_k1[l], lam_q2[l], lam_k2[l]]).astype(F32)
        mix_d = _dense_attn(qd, kd, vd, sg, 3 * BRANCH_W, batch, seq, diff=True, lam_vecs=lam_vecs,
                            subln_col=subln_d[l].reshape(HEAD_DIM, 1), lam_init=lam_init)
        pending = ((mix_a, mix_b, mix_c, mix_d), w_outT)
    out = _out_proj_final(res, *pending, depth - 1, token_major, final_g.reshape(d, 1))
    return out.reshape(batch, seq, d)
```
